```python
import jax, jax.numpy as jnp
from jax import lax
import numpy as np

D_MODEL = 1024
BATCH = 8
SEQ = 4096
DEPTH = 4

MIX_WIDTH = D_MODEL
HG_WIDTH = MIX_WIDTH // 2
LRU_WIDTH = MIX_WIDTH - HG_WIDTH
HG_HEADS = 4
HG_HEAD_DIM = HG_WIDTH // HG_HEADS
HG_CHUNK = 32
LB_FLOOR = 1e-30
LRU_HEADS = 8
LRU_HEAD_DIM = LRU_WIDTH // LRU_HEADS
LRU_C = 8.0
CONV_WIDTH = 4
CONV_PAD = (2, 1)
N_EXPERTS = 32
TOP_K = 4
D_EXPERT = D_MODEL
SWIGLU_LIMIT = 7.0
SWIGLU_ALPHA = 1.702
EXPERT_BLOCK = 256
DN_ALPHA = float((2 * DEPTH) ** 0.25)
DN_BETA = float((8 * DEPTH) ** -0.25)
LN_EPS = 1e-5
RMS_EPS = 1e-6
IN_COLS = 5 * HG_WIDTH + 2 * LRU_WIDTH

kernel_name = "hybrid_hgrn2_rglru_moe_deepnorm_encoder"


def _layernorm(t, g, b):
    tf = t.astype(jnp.float32)
    mu = jnp.mean(tf, axis=-1, keepdims=True)
    var = jnp.mean(jnp.square(tf - mu), axis=-1, keepdims=True)
    y = (tf - mu) * lax.rsqrt(var + LN_EPS) * g.astype(jnp.float32) + b.astype(jnp.float32)
    return y.astype(t.dtype)


def _rmsnorm(t, g):
    return t * lax.rsqrt(jnp.mean(jnp.square(t), axis=-1, keepdims=True) + RMS_EPS) * g


def _hgrn2_chunked(q, k, v, logf):
    b, sl, h, dk = q.shape
    dv = v.shape[-1]
    n = sl // HG_CHUNK
    blk = lambda t: t.reshape(b, n, HG_CHUNK, h, t.shape[-1])
    q, k, v, logf = blk(q), blk(k), blk(v), blk(logf)
    g = jnp.cumsum(logf, axis=2)
    g_ref = g[:, :, HG_CHUNK // 2 - 1:HG_CHUNK // 2]
    g_last = g[:, :, -1:]
    tri = jnp.tril(jnp.ones((HG_CHUNK, HG_CHUNK), dtype=bool))
    scores = jnp.einsum('bnthd,bnshd->bnhts', q * jnp.exp(g - g_ref), k * jnp.exp(g_ref - g))
    scores = jnp.where(tri, scores, 0.0)
    o_intra = jnp.einsum('bnhts,bnshv->bnthv', scores, v)
    chunk_kv = jnp.einsum('bnshd,bnshv->nbhdv', k * jnp.exp(g_last - g), v)
    chunk_decay = jnp.moveaxis(jnp.exp(g_last[:, :, 0]), 1, 0)

    def step(state, inp):
        dec, kv = inp
        return state * dec[..., None] + kv, state

    _, s_in = lax.scan(step, jnp.zeros((b, h, dk, dv), q.dtype), (chunk_decay, chunk_kv))
    o_inter = jnp.einsum('bnthd,nbhdv->bnthv', q * jnp.exp(g), s_in)
    return (o_intra + o_inter).reshape(b, sl, h, dv)


def _hgrn2_mixer(zq, zi, zf, zb, zg, lb, norm_g):
    b, sl, _ = zq.shape
    heads = lambda t: t.reshape(b, sl, HG_HEADS, HG_HEAD_DIM)
    q = heads(jax.nn.silu(zq.astype(jnp.float32)))
    v = heads(zi.astype(jnp.float32))

    def gates(z, lb_d):
        z = z.astype(jnp.float32)
        log_lb = jnp.log(jnp.maximum(lb_d, LB_FLOOR))
        logf = jnp.logaddexp(log_lb, jnp.log1p(-lb_d) + jax.nn.log_sigmoid(z))
        k = (1.0 - lb_d) * jax.nn.sigmoid(-z)
        return heads(k), heads(logf)

    k_f, logf_f = gates(zf, lb[0])
    k_b, logf_b = gates(zb, lb[1])
    o_f = _hgrn2_chunked(q, k_f, v, logf_f)
    rev = lambda t: jnp.flip(t, axis=1)
    o_b = rev(_hgrn2_chunked(rev(q), rev(k_b), rev(v), rev(logf_b)))
    o = _rmsnorm(o_f + o_b, norm_g.astype(jnp.float32).reshape(HG_HEADS, HG_HEAD_DIM))
    o = o.reshape(b, sl, HG_WIDTH) * jax.nn.silu(zg.astype(jnp.float32))
    return o


def _rg_lru(xc, wa, ba, wx, bx, lam, reverse):
    b, sl, w = xc.shape
    xh = xc.reshape(b, sl, LRU_HEADS, LRU_HEAD_DIM)
    r = jax.nn.sigmoid(jnp.einsum('bshi,hij->bshj', xh, wa.astype(jnp.float32)).reshape(b, sl, w)
                       + ba.astype(jnp.float32))
    i = jax.nn.sigmoid(jnp.einsum('bshi,hij->bshj', xh, wx.astype(jnp.float32)).reshape(b, sl, w)
                       + bx.astype(jnp.float32))
    log_a = -LRU_C * r * jax.nn.softplus(-lam.astype(jnp.float32))
    a = jnp.exp(log_a)
    u = jnp.sqrt(jnp.maximum(-jnp.expm1(2.0 * log_a), 0.0)) * (i * xc)

    def combine(e1, e2):
        a1, b1 = e1
        a2, b2 = e2
        return a1 * a2, a2 * b1 + b2

    _, h = lax.associative_scan(combine, (a, u), axis=1, reverse=reverse)
    return h


def _griffin_mixer(zx, zy, conv_w, conv_b, wa, ba, wx, bx, lam, norm_g):
    xc = lax.conv_general_dilated(zx, conv_w[:, None, :], window_strides=(1,), padding=[CONV_PAD],
                                  dimension_numbers=('NWC', 'WIO', 'NWC'),
                                  feature_group_count=LRU_WIDTH)
    xc = (xc + conv_b).astype(jnp.float32)
    h = (_rg_lru(xc, wa[0], ba[0], wx[0], bx[0], lam[0], False)
         + _rg_lru(xc, wa[1], ba[1], wx[1], bx[1], lam[1], True))
    return _rmsnorm(h, norm_g.astype(jnp.float32)) * jax.nn.gelu(zy.astype(jnp.float32))


def _moe(x2, router_w, router_b, w_gu, b_gu, w_dn, b_dn):
    t, d = x2.shape
    dt = x2.dtype
    logits = jnp.einsum('td,de->te', x2, router_w).astype(jnp.float32) + router_b.astype(jnp.float32)
    top_logits, top_idx = lax.top_k(logits, TOP_K)
    gates = jax.nn.softmax(top_logits, axis=-1)
    n_assign = t * TOP_K
    flat_e = top_idx.reshape(-1).astype(jnp.int32)
    flat_tok = jnp.arange(n_assign, dtype=jnp.int32) // TOP_K
    flat_w = gates.reshape(-1)
    order = jnp.argsort(flat_e)
    s_e, s_tok, s_w = flat_e[order], flat_tok[order], flat_w[order]
    counts = jnp.bincount(flat_e, length=N_EXPERTS).astype(jnp.int32)
    starts = jnp.cumsum(counts) - counts
    padded = ((counts + EXPERT_BLOCK - 1) // EXPERT_BLOCK) * EXPERT_BLOCK
    pends = jnp.cumsum(padded)
    pstarts = pends - padded
    dest = pstarts[s_e] + jnp.arange(n_assign, dtype=jnp.int32) - starts[s_e]
    n_rows = ((n_assign + EXPERT_BLOCK - 1) // EXPERT_BLOCK) * EXPERT_BLOCK + N_EXPERTS * EXPERT_BLOCK
    n_blocks = n_rows // EXPERT_BLOCK
    row_tok = jnp.zeros((n_rows,), jnp.int32).at[dest].set(s_tok).reshape(n_blocks, EXPERT_BLOCK)
    row_w = jnp.zeros((n_rows,), dt).at[dest].set(s_w.astype(dt)).reshape(n_blocks, EXPERT_BLOCK)
    blk_start = jnp.arange(n_blocks, dtype=jnp.int32) * EXPERT_BLOCK
    blk_e = jnp.minimum(jnp.searchsorted(pends, blk_start, side='right'), N_EXPERTS - 1).astype(jnp.int32)

    def body(out, inp):
        tok, wgt, e = inp
        xb = x2[tok]
        gu = xb @ w_gu[e] + b_gu[e]
        gate = jnp.minimum(gu[:, :D_EXPERT], SWIGLU_LIMIT)
        up = jnp.clip(gu[:, D_EXPERT:], -SWIGLU_LIMIT, SWIGLU_LIMIT)
        hid = (up + 1.0) * (gate * jax.nn.sigmoid(SWIGLU_ALPHA * gate))
        y = hid @ w_dn[e] + b_dn[e]
        return out.at[tok].add(y * wgt[:, None]), None

    out, _ = lax.scan(body, jnp.zeros_like(x2), (row_tok, row_w, blk_e))
    return out


def setup_inputs(seed: int = 0) -> dict:
    key = jax.random.key(seed)
    ks = jax.random.split(key, 24)
    f32 = jnp.float32
    L = DEPTH
    nrm = lambda k, shape, scale: jax.random.normal(k, shape, f32) * scale
    a0 = jax.random.uniform(ks[10], (L, 2, LRU_WIDTH), f32, 0.9, 0.999)
    return {
        "x": nrm(ks[0], (BATCH, SEQ, D_MODEL), 1.0),
        "w_in": nrm(ks[1], (L, D_MODEL, IN_COLS), D_MODEL ** -0.5),
        "hg_lb": nrm(ks[2], (L, 2, HG_WIDTH), 0.5),
        "hg_norm": 1.0 + nrm(ks[3], (L, HG_WIDTH), 0.02),
        "lru_conv_w": nrm(ks[4], (L, CONV_WIDTH, LRU_WIDTH), CONV_WIDTH ** -0.5),
        "lru_conv_b": nrm(ks[5], (L, LRU_WIDTH), 0.02),
        "lru_wa": nrm(ks[6], (L, 2, LRU_HEADS, LRU_HEAD_DIM, LRU_HEAD_DIM), LRU_HEAD_DIM ** -0.5),
        "lru_ba": nrm(ks[7], (L, 2, LRU_WIDTH), 0.02),
        "lru_wx": nrm(ks[8], (L, 2, LRU_HEADS, LRU_HEAD_DIM, LRU_HEAD_DIM), LRU_HEAD_DIM ** -0.5),
        "lru_bx": nrm(ks[9], (L, 2, LRU_WIDTH), 0.02),
        "lru_lambda": jnp.log(a0) - jnp.log1p(-a0),
        "lru_norm": 1.0 + nrm(ks[11], (L, LRU_WIDTH), 0.02),
        "w_out": nrm(ks[12], (L, MIX_WIDTH, D_MODEL), MIX_WIDTH ** -0.5 * DN_BETA),
        "ln1_g": 1.0 + nrm(ks[13], (L, D_MODEL), 0.02),
        "ln1_b": nrm(ks[14], (L, D_MODEL), 0.02),
        "router_w": nrm(ks[15], (L, D_MODEL, N_EXPERTS), D_MODEL ** -0.5),
        "router_b": nrm(ks[16], (L, N_EXPERTS), 0.01),
        "w_gate_up": nrm(ks[17], (L, N_EXPERTS, D_MODEL, 2 * D_EXPERT), D_MODEL ** -0.5),
        "b_gate_up": nrm(ks[18], (L, N_EXPERTS, 2 * D_EXPERT), 0.02),
        "w_down": nrm(ks[19], (L, N_EXPERTS, D_EXPERT, D_MODEL), D_EXPERT ** -0.5 * DN_BETA),
        "b_down": nrm(ks[20], (L, N_EXPERTS, D_MODEL), 0.02),
        "ln2_g": 1.0 + nrm(ks[21], (L, D_MODEL), 0.02),
        "ln2_b": nrm(ks[22], (L, D_MODEL), 0.02),
    }


def reference(x, w_in, hg_lb, hg_norm, lru_conv_w, lru_conv_b, lru_wa, lru_ba, lru_wx, lru_bx,
              lru_lambda, lru_norm, w_out, ln1_g, ln1_b, router_w, router_b, w_gate_up, b_gate_up,
              w_down, b_down, ln2_g, ln2_b):
    dt = x.dtype
    b, sl, d = x.shape
    p = jax.nn.softmax(hg_lb.astype(jnp.float32), axis=0)
    lower_bounds = jnp.clip(jnp.cumsum(p, axis=0) - p[0:1], 0.0, 1.0 - 1e-6)
    split_at = [HG_WIDTH * j for j in range(1, 6)] + [5 * HG_WIDTH + LRU_WIDTH]
    for l in range(DEPTH):
        proj = jnp.einsum('bsd,dc->bsc', x, w_in[l])
        zq, zi, zf, zb, zg, zx, zy = jnp.split(proj, split_at, axis=-1)
        o_hg = _hgrn2_mixer(zq, zi, zf, zb, zg, lower_bounds[l], hg_norm[l])
        o_lru = _griffin_mixer(zx, zy, lru_conv_w[l], lru_conv_b[l], lru_wa[l], lru_ba[l],
                               lru_wx[l], lru_bx[l], lru_lambda[l], lru_norm[l])
        mix = jnp.concatenate([o_hg, o_lru], axis=-1).astype(dt)
        y = jnp.einsum('bsc,cd->bsd', mix, w_out[l])
        x = _layernorm(DN_ALPHA * x + y, ln1_g[l], ln1_b[l])
        m = _moe(x.reshape(b * sl, d), router_w[l], router_b[l], w_gate_up[l], b_gate_up[l],
                 w_down[l], b_down[l]).reshape(b, sl, d)
        x = _layernorm(DN_ALPHA * x + m, ln2_g[l], ln2_b[l])
    return x
```

```python
import functools

import jax
import jax.numpy as jnp
from jax import lax
from jax.experimental import pallas as pl
from jax.experimental.pallas import tpu as pltpu

F32 = jnp.float32
BF16 = jnp.bfloat16

HG_HEADS = 4
HG_CHUNK = 32
LB_FLOOR = 1e-30
LRU_HEADS = 8
LRU_C = 8.0
CONV_WIDTH = 4
TOP_K = 4
SWIGLU_LIMIT = 7.0
SWIGLU_ALPHA = 1.702
LN_EPS = 1e-5
RMS_EPS = 1e-6

LANES = 128
SUBLANES = 8
MXU_DIM = 256
VMEM_LIMIT_BYTES = 56 * 1024 * 1024

NT_DIMS = (((1,), (1,)), ((), ()))
TN_DIMS = (((0,), (0,)), ((), ()))


def _params(*semantics):
    return pltpu.CompilerParams(dimension_semantics=semantics, vmem_limit_bytes=VMEM_LIMIT_BYTES)


def _bdot(a, b):
    return jnp.dot(a.astype(BF16), b.astype(BF16), preferred_element_type=F32)


def _bdot_general(a, b, dims):
    return lax.dot_general(a.astype(BF16), b.astype(BF16), dims, preferred_element_type=F32)


def _layernorm(t, g, b):
    mu = jnp.mean(t, axis=-1, keepdims=True)
    c = t - mu
    var = jnp.mean(c * c, axis=-1, keepdims=True)
    return c * lax.rsqrt(var + LN_EPS) * g + b


def _in_proj_kernel(x_ref, w_ref, o_ref):
    o_ref[...] = jnp.dot(x_ref[...].astype(BF16), w_ref[...], preferred_element_type=F32)


def _in_proj(x2, w_in_bf16, layer):
    t, d = x2.shape
    n = w_in_bf16.shape[-1]
    tm = min(1024, t)
    tn = n // 2
    return pl.pallas_call(
        _in_proj_kernel,
        grid=(t // tm, n // tn),
        in_specs=[
            pl.BlockSpec((tm, d), lambda i, j: (i, 0)),
            pl.BlockSpec((None, d, tn), lambda i, j: (layer, 0, j)),
        ],
        out_specs=pl.BlockSpec((tm, tn), lambda i, j: (i, j)),
        out_shape=jax.ShapeDtypeStruct((t, n), F32),
        compiler_params=_params("parallel", "parallel"),
        name="in_proj",
    )(x2, w_in_bf16)


def _cumsum_rows(x):
    n = x.shape[0]
    row = lax.broadcasted_iota(jnp.int32, x.shape, 0)
    s = 1
    while s < n:
        x = x + jnp.where(row >= s, pltpu.roll(x, s, 0), 0.0)
        s *= 2
    return x


def _hg_chunk(zq, zv, zf, lb, log_lb, log1m_lb, st_ref, o_ref, r0, reverse):
    c = HG_CHUNK
    q = zq * jax.nn.sigmoid(zq)
    e = jnp.exp(-jnp.abs(zf))
    log_sig = jnp.minimum(zf, 0.0) - jnp.log1p(e)
    sig_neg = jnp.where(zf >= 0.0, e, 1.0) / (1.0 + e)
    b = log1m_lb + log_sig
    logf = jnp.maximum(log_lb, b) + jnp.log1p(jnp.exp(-jnp.abs(log_lb - b)))
    k = (1.0 - lb) * sig_neg

    p = _cumsum_rows(logf)
    g_last = p[c - 1:c]
    if reverse:
        g = g_last - p + logf
        g_ref = g[c // 2:c // 2 + 1]
    else:
        g = p
        g_ref = g[c // 2 - 1:c // 2]
    qg = q * jnp.exp(g - g_ref)
    kg = k * jnp.exp(g_ref - g)
    kl = kg * jnp.exp(g_last - g_ref)
    qe = qg * jnp.exp(g_ref)
    dec = jnp.exp(g_last)

    row = lax.broadcasted_iota(jnp.int32, (c, c), 0)
    col = lax.broadcasted_iota(jnp.int32, (c, c), 1)
    keep = (col >= row) if reverse else (col <= row)
    hd = zq.shape[1] // HG_HEADS
    for h in range(HG_HEADS):
        sl = slice(h * hd, (h + 1) * hd)
        scores = jnp.where(keep, _bdot_general(qg[:, sl], kg[:, sl], NT_DIMS), 0.0)
        st = st_ref[h]
        o = _bdot(scores, zv[:, sl]) + _bdot_general(qe[:, sl], st, NT_DIMS)
        o_ref[pl.ds(r0, c), sl] = o
        st_ref[h] = st * dec[:, sl] + _bdot_general(zv[:, sl], kl[:, sl], TN_DIMS)


def _hgrn2_kernel(fqif_ref, bqi_ref, bzb_ref, lb_ref, of_ref, ob_ref, stf_ref, stb_ref):
    @pl.when(pl.program_id(1) == 0)
    def _():
        stf_ref[...] = jnp.zeros_like(stf_ref)
        stb_ref[...] = jnp.zeros_like(stb_ref)

    w = lb_ref.shape[1]
    lb = lb_ref[...]
    log_lb = jnp.log(jnp.maximum(lb, LB_FLOOR))
    log1m_lb = jnp.log1p(-lb)
    n_chunks = of_ref.shape[0] // HG_CHUNK

    def body(ci, carry):
        r0 = pl.multiple_of(ci * HG_CHUNK, HG_CHUNK)
        rows = pl.ds(r0, HG_CHUNK)
        _hg_chunk(fqif_ref[rows, 0:w], fqif_ref[rows, w:2 * w], fqif_ref[rows, 2 * w:3 * w],
                  lb[0:1], log_lb[0:1], log1m_lb[0:1], stf_ref, of_ref, r0, False)
        rb = pl.multiple_of((n_chunks - 1 - ci) * HG_CHUNK, HG_CHUNK)
        rows_b = pl.ds(rb, HG_CHUNK)
        _hg_chunk(bqi_ref[rows_b, 0:w], bqi_ref[rows_b, w:2 * w], bzb_ref[rows_b, :],
                  lb[1:2], log_lb[1:2], log1m_lb[1:2], stb_ref, ob_ref, rb, True)
        return carry

    lax.fori_loop(0, n_chunks, body, 0)


def _hgrn2(proj3, lower_bounds, layer, hg_width):
    bsz, s, _ = proj3.shape
    w = hg_width
    ts = min(256, s)
    ns = s // ts
    hd = w // HG_HEADS
    out = jax.ShapeDtypeStruct((bsz, s, w), F32)
    return pl.pallas_call(
        _hgrn2_kernel,
        grid=(bsz, ns),
        in_specs=[
            pl.BlockSpec((None, ts, 3 * w), lambda b, j: (b, j, 0)),
            pl.BlockSpec((None, ts, 2 * w), lambda b, j: (b, ns - 1 - j, 0)),
            pl.BlockSpec((None, ts, w), lambda b, j: (b, ns - 1 - j, 3)),
            pl.BlockSpec((None, 2, w), lambda b, j: (layer, 0, 0)),
        ],
        out_specs=[
            pl.BlockSpec((None, ts, w), lambda b, j: (b, j, 0)),
            pl.BlockSpec((None, ts, w), lambda b, j: (b, ns - 1 - j, 0)),
        ],
        out_shape=[out, out],
        scratch_shapes=[pltpu.VMEM((HG_HEADS, hd, hd), F32), pltpu.VMEM((HG_HEADS, hd, hd), F32)],
        compiler_params=_params("parallel", "arbitrary"),
        name="hgrn2",
    )(proj3, proj3, proj3, lower_bounds)


def _lin_scan(a, u, reverse):
    n = a.shape[0]
    row = lax.broadcasted_iota(jnp.int32, a.shape, 0)
    s = 1
    while s < n:
        shift = (n - s) if reverse else s
        m = (row < n - s) if reverse else (row >= s)
        a_sh = jnp.where(m, pltpu.roll(a, shift, 0), 1.0)
        u_sh = jnp.where(m, pltpu.roll(u, shift, 0), 0.0)
        u = u + a * u_sh
        a = a * a_sh
        s *= 2
    return a, u


def _griffin_dir(main_ref, prev_ref, next_ref, is_first, is_last, ext_ref, cw, cb, wg_ref, bias, sp,
                 carry_ref, h_ref, reverse):
    ts, w = main_ref.shape
    halo = SUBLANES
    ext_ref[0:halo, :] = jnp.where(is_first, 0.0, prev_ref[...])
    ext_ref[halo:halo + ts, :] = main_ref[...]
    ext_ref[halo + ts:2 * halo + ts, :] = jnp.where(is_last, 0.0, next_ref[...])
    xc = cb
    for j in range(CONV_WIDTH):
        xc = xc + cw[j:j + 1] * ext_ref[halo - 2 + j:halo - 2 + j + ts, :]

    half = w // 2
    pre = [_bdot(xc[:, i * half:(i + 1) * half], wg_ref[i]) for i in range(2)]
    r_pre = jnp.concatenate([pre[0][:, :half], pre[1][:, :half]], axis=1) + bias[0:1]
    i_pre = jnp.concatenate([pre[0][:, half:], pre[1][:, half:]], axis=1) + bias[1:2]
    r = jax.nn.sigmoid(r_pre)
    ig = jax.nn.sigmoid(i_pre)
    log_a = (-LRU_C) * r * sp
    a = jnp.exp(log_a)
    u = jnp.sqrt(jnp.maximum(1.0 - a * a, 0.0)) * (ig * xc)
    dec, h = _lin_scan(a, u, reverse)
    h = h + dec * carry_ref[0:1, :]
    h_ref[...] = h
    carry_ref[0:1, :] = h[0:1] if reverse else h[ts - 1:ts]


def _griffin_kernel(fm_ref, fp_ref, fn_ref, bm_ref, bp_ref, bn_ref, cw_ref, cb_ref, wg_ref, bias_ref, lam_ref,
                    hf_ref, hb_ref, ext_ref, cf_ref, cbk_ref):
    j = pl.program_id(1)
    ns = pl.num_programs(1)

    @pl.when(j == 0)
    def _():
        cf_ref[...] = jnp.zeros_like(cf_ref)
        cbk_ref[...] = jnp.zeros_like(cbk_ref)

    cw = cw_ref[...]
    cb = cb_ref[...]
    sp = jax.nn.softplus(-lam_ref[...])
    _griffin_dir(fm_ref, fp_ref, fn_ref, j == 0, j == ns - 1, ext_ref, cw, cb, wg_ref.at[0], bias_ref[0], sp[0:1],
                 cf_ref, hf_ref, False)
    _griffin_dir(bm_ref, bp_ref, bn_ref, j == ns - 1, j == 0, ext_ref, cw, cb, wg_ref.at[1], bias_ref[1], sp[1:2],
                 cbk_ref, hb_ref, True)


def _griffin(proj3, conv_w, conv_b3, wg, gate_bias, lam, layer, lru_width, col_block):
    bsz, s, _ = proj3.shape
    w = lru_width
    ts = min(256, s)
    ns = s // ts
    tb = ts // SUBLANES
    nb8 = s // SUBLANES
    out = jax.ShapeDtypeStruct((bsz, s, w), F32)
    halo = (None, SUBLANES, w)
    return pl.pallas_call(
        _griffin_kernel,
        grid=(bsz, ns),
        in_specs=[
            pl.BlockSpec((None, ts, w), lambda b, j: (b, j, col_block)),
            pl.BlockSpec(halo, lambda b, j: (b, jnp.maximum(j * tb - 1, 0), col_block)),
            pl.BlockSpec(halo, lambda b, j: (b, jnp.minimum((j + 1) * tb, nb8 - 1), col_block)),
            pl.BlockSpec((None, ts, w), lambda b, j: (b, ns - 1 - j, col_block)),
            pl.BlockSpec(halo, lambda b, j: (b, jnp.maximum((ns - 1 - j) * tb - 1, 0), col_block)),
            pl.BlockSpec(halo, lambda b, j: (b, jnp.minimum((ns - j) * tb, nb8 - 1), col_block)),
            pl.BlockSpec((None, CONV_WIDTH, w), lambda b, j: (layer, 0, 0)),
            pl.BlockSpec((None, 1, w), lambda b, j: (layer, 0, 0)),
            pl.BlockSpec((None, 2, 2, w // 2, w), lambda b, j: (layer, 0, 0, 0, 0)),
            pl.BlockSpec((None, 2, 2, w), lambda b, j: (layer, 0, 0, 0)),
            pl.BlockSpec((None, 2, w), lambda b, j: (layer, 0, 0)),
        ],
        out_specs=[
            pl.BlockSpec((None, ts, w), lambda b, j: (b, j, 0)),
            pl.BlockSpec((None, ts, w), lambda b, j: (b, ns - 1 - j, 0)),
        ],
        out_shape=[out, out],
        scratch_shapes=[
            pltpu.VMEM((ts + 2 * SUBLANES, w), F32),
            pltpu.VMEM((SUBLANES, w), F32),
            pltpu.VMEM((SUBLANES, w), F32),
        ],
        compiler_params=_params("parallel", "arbitrary"),
        name="griffin",
    )(proj3, proj3, proj3, proj3, proj3, proj3, conv_w, conv_b3, wg, gate_bias, lam)


def _gelu_tanh(x):
    return 0.5 * x * (1.0 + jnp.tanh(0.7978845608028654 * (x + 0.044715 * x * x * x)))


def _mix_kernel(x_ref, of_ref, ob_ref, zg_ref, hf_ref, hb_ref, zy_ref, wout_ref, hgn_ref, lrn_ref, g1_ref, b1_ref,
                rw_ref, rb_ref, x1_ref, ridx_ref, rgate_ref, *, alpha):
    o = of_ref[...] + ob_ref[...]
    w = o.shape[1]
    hd = w // HG_HEADS
    parts = []
    for h in range(HG_HEADS):
        oh = o[:, h * hd:(h + 1) * hd]
        parts.append(oh * lax.rsqrt(jnp.mean(oh * oh, axis=-1, keepdims=True) + RMS_EPS))
    zg = zg_ref[...]
    o_hg = jnp.concatenate(parts, axis=1) * hgn_ref[...] * (zg * jax.nn.sigmoid(zg))
    hh = hf_ref[...] + hb_ref[...]
    o_lru = hh * lax.rsqrt(jnp.mean(hh * hh, axis=-1, keepdims=True) + RMS_EPS) * lrn_ref[...]
    o_lru = o_lru * _gelu_tanh(zy_ref[...])
    y = _bdot(o_hg, wout_ref[0:w, :]) + _bdot(o_lru, wout_ref[w:, :])
    x1 = _layernorm(alpha * x_ref[...] + y, g1_ref[...], b1_ref[...])
    x1_ref[...] = x1

    logits = jnp.dot(x1, rw_ref[...], precision=lax.Precision.HIGHEST, preferred_element_type=F32) + rb_ref[...]
    n_exp = logits.shape[1]
    eid = lax.broadcasted_iota(jnp.int32, logits.shape, 1)
    vals, idxs = [], []
    for _ in range(TOP_K):
        m = jnp.max(logits, axis=-1, keepdims=True)
        idx = jnp.min(jnp.where(logits == m, eid, n_exp), axis=-1, keepdims=True)
        vals.append(m)
        idxs.append(idx)
        logits = jnp.where(eid == idx, -jnp.inf, logits)
    exps = [jnp.exp(v - vals[0]) for v in vals]
    denom = exps[0] + exps[1] + exps[2] + exps[3]
    lane = lax.broadcasted_iota(jnp.int32, ridx_ref.shape, 1)
    ridx = jnp.zeros(ridx_ref.shape, jnp.int32)
    rgate = jnp.zeros(rgate_ref.shape, F32)
    for kk in range(TOP_K):
        ridx = jnp.where(lane == kk, idxs[kk], ridx)
        rgate = jnp.where(lane == kk, exps[kk] / denom, rgate)
    ridx_ref[...] = ridx
    rgate_ref[...] = rgate


def _mix(x2, o_f, o_b, h_f, h_b, proj, w_out_bf16, hg_norm3, lru_norm3, ln_g3, ln_b3, router_w, router_b3, layer,
         alpha, zg_block, zy_block):
    t, d = x2.shape
    w = o_f.shape[1]
    n_exp = router_w.shape[-1]
    tm = min(512, t)
    row_blk = lambda width: pl.BlockSpec((tm, width), lambda i: (i, 0))
    vec = lambda width: pl.BlockSpec((None, 1, width), lambda i: (layer, 0, 0))
    return pl.pallas_call(
        functools.partial(_mix_kernel, alpha=alpha),
        grid=(t // tm,),
        in_specs=[
            row_blk(d), row_blk(w), row_blk(w),
            pl.BlockSpec((tm, w), lambda i: (i, zg_block)),
            row_blk(w), row_blk(w),
            pl.BlockSpec((tm, w), lambda i: (i, zy_block)),
            pl.BlockSpec((None, 2 * w, d), lambda i: (layer, 0, 0)),
            vec(w), vec(w), vec(d), vec(d),
            pl.BlockSpec((None, d, n_exp), lambda i: (layer, 0, 0)),
            vec(n_exp),
        ],
        out_specs=[row_blk(d), row_blk(LANES), row_blk(LANES)],
        out_shape=[
            jax.ShapeDtypeStruct((t, d), F32),
            jax.ShapeDtypeStruct((t, LANES), jnp.int32),
            jax.ShapeDtypeStruct((t, LANES), F32),
        ],
        compiler_params=_params("parallel"),
        name="mix_ln_router",
    )(x2, o_f, o_b, proj, h_f, h_b, proj, w_out_bf16, hg_norm3, lru_norm3, ln_g3, ln_b3, router_w, router_b3)


def _routing_tables(ridx, n_exp, block_rows):
    t = ridx.shape[0]
    flat_e = ridx[:, :TOP_K].reshape(-1)
    onehot = (flat_e[:, None] == jnp.arange(n_exp, dtype=jnp.int32)[None, :]).astype(jnp.int32)
    csum = jnp.cumsum(onehot, axis=0)
    rank = jnp.sum(jnp.where(onehot > 0, csum, 0), axis=1) - 1
    counts = csum[-1]
    padded = ((counts + block_rows - 1) // block_rows) * block_rows
    pends = jnp.cumsum(padded)
    pstarts = pends - padded
    dest = (jnp.sum(jnp.where(onehot > 0, pstarts[None, :], 0), axis=1) + rank).astype(jnp.int32)
    n_rows = t * TOP_K + n_exp * block_rows
    n_blocks = n_rows // block_rows
    blk_start = jnp.arange(n_blocks, dtype=jnp.int32) * block_rows
    blk_e = jnp.minimum(jnp.searchsorted(pends, blk_start, side="right"), n_exp - 1).astype(jnp.int32)
    n_used = (pends[-1] // block_rows).astype(jnp.int32).reshape(1)
    return dest, counts.astype(jnp.int32), pstarts.astype(jnp.int32), blk_e, n_used, n_rows


def _dispatch_kernel(cnt_ref, pst_ref, dest_ref, x_ref, xs_ref, zero_ref, sem, *, block_rows):
    tm = x_ref.shape[0]

    def issue(r, carry):
        for kk in range(TOP_K):
            d = dest_ref[r * TOP_K + kk]
            pltpu.make_async_copy(x_ref.at[pl.ds(r, 1)], xs_ref.at[pl.ds(d, 1)], sem).start()
        return carry

    lax.fori_loop(0, tm, issue, 0)
    for _ in range(TOP_K):
        pltpu.make_async_copy(x_ref, xs_ref.at[pl.ds(0, tm)], sem).wait()

    @pl.when(pl.program_id(0) == pl.num_programs(0) - 1)
    def _():
        zero_ref[...] = jnp.zeros_like(zero_ref)
        zrow = zero_ref.at[pl.ds(0, 1)]
        n_exp = cnt_ref.shape[0]
        last = n_exp - 1
        cnt_last = cnt_ref[last]
        used_rows = pst_ref[last] + cnt_last + (block_rows - cnt_last % block_rows) % block_rows
        n_tail = (xs_ref.shape[0] - used_rows) // block_rows

        def tail_copy(b):
            start = pl.multiple_of(used_rows + b * block_rows, block_rows)
            return pltpu.make_async_copy(zero_ref, xs_ref.at[pl.ds(start, block_rows)], sem)

        def tail_start(b, c2):
            tail_copy(b).start()
            return c2

        def tail_wait(b, c2):
            tail_copy(b).wait()
            return c2

        lax.fori_loop(0, n_tail, tail_start, 0)
        lax.fori_loop(0, n_tail, tail_wait, 0)

        def per_expert(e, carry):
            cnt = cnt_ref[e]
            first = pst_ref[e] + cnt
            n_pad = (block_rows - cnt % block_rows) % block_rows

            def pad_start(r, c2):
                pltpu.make_async_copy(zrow, xs_ref.at[pl.ds(first + r, 1)], sem).start()
                return c2

            def pad_wait(r, c2):
                pltpu.make_async_copy(zrow, xs_ref.at[pl.ds(first + r, 1)], sem).wait()
                return c2

            lax.fori_loop(0, n_pad, pad_start, 0)
            lax.fori_loop(0, n_pad, pad_wait, 0)
            return carry

        lax.fori_loop(0, cnt_ref.shape[0], per_expert, 0)


def _dispatch(x1, dest, counts, pstarts, n_rows, block_rows):
    t, d = x1.shape
    tm = min(256, t)
    return pl.pallas_call(
        functools.partial(_dispatch_kernel, block_rows=block_rows),
        grid_spec=pltpu.PrefetchScalarGridSpec(
            num_scalar_prefetch=2,
            grid=(t // tm,),
            in_specs=[
                pl.BlockSpec((tm * TOP_K,), lambda i, c, p: (i,), memory_space=pltpu.SMEM),
                pl.BlockSpec((tm, d), lambda i, c, p: (i, 0)),
            ],
            out_specs=pl.BlockSpec(memory_space=pl.ANY),
            scratch_shapes=[pltpu.VMEM((block_rows, d), F32), pltpu.SemaphoreType.DMA],
        ),
        out_shape=jax.ShapeDtypeStruct((n_rows, d), F32),
        compiler_params=_params("arbitrary"),
        name="moe_dispatch",
    )(counts, pstarts, dest, x1)


def _moe_ffn_kernel(blk_e_ref, n_used_ref, xs_ref, wgu_ref, bgu_ref, wdn_ref, bdn_ref, ys_ref):
    @pl.when(pl.program_id(0) < n_used_ref[0])
    def _():
        de = wdn_ref.shape[0]
        gu = jnp.dot(xs_ref[...].astype(BF16), wgu_ref[...], preferred_element_type=F32) + bgu_ref[...]
        gate = jnp.minimum(gu[:, :de], SWIGLU_LIMIT)
        up = jnp.clip(gu[:, de:], -SWIGLU_LIMIT, SWIGLU_LIMIT)
        hid = (up + 1.0) * (gate * jax.nn.sigmoid(SWIGLU_ALPHA * gate))
        ys_ref[...] = jnp.dot(hid.astype(BF16), wdn_ref[...], preferred_element_type=F32) + bdn_ref[...]

    @pl.when(pl.program_id(0) >= n_used_ref[0])
    def _():
        ys_ref[...] = jnp.zeros_like(ys_ref)


def _moe_ffn(xs, blk_e, n_used, w_gu_bf16, b_gu4, w_dn_bf16, b_dn4, layer, block_rows):
    n_rows, d = xs.shape
    de = w_dn_bf16.shape[2]
    n_blocks = n_rows // block_rows
    row_map = lambda i, be, nu: (jnp.minimum(i, nu[0] - 1), 0)
    return pl.pallas_call(
        _moe_ffn_kernel,
        grid_spec=pltpu.PrefetchScalarGridSpec(
            num_scalar_prefetch=2,
            grid=(n_blocks,),
            in_specs=[
                pl.BlockSpec((block_rows, d), row_map),
                pl.BlockSpec((None, None, d, 2 * de), lambda i, be, nu: (layer, be[i], 0, 0)),
                pl.BlockSpec((None, None, 1, 2 * de), lambda i, be, nu: (layer, be[i], 0, 0)),
                pl.BlockSpec((None, None, de, d), lambda i, be, nu: (layer, be[i], 0, 0)),
                pl.BlockSpec((None, None, 1, d), lambda i, be, nu: (layer, be[i], 0, 0)),
            ],
            out_specs=pl.BlockSpec((block_rows, d), lambda i, be, nu: (i, 0)),
        ),
        out_shape=jax.ShapeDtypeStruct((n_rows, d), F32),
        compiler_params=_params("arbitrary"),
        name="moe_ffn",
    )(blk_e, n_used, xs, w_gu_bf16, b_gu4, w_dn_bf16, b_dn4)


def _combine_kernel(dest_ref, gate_ref, x1_ref, ys_ref, g2_ref, b2_ref, o_ref, buf_ref, sem, *, alpha):
    tm = x1_ref.shape[0]

    def issue(r, carry):
        for kk in range(TOP_K):
            d = dest_ref[r * TOP_K + kk]
            pltpu.make_async_copy(ys_ref.at[pl.ds(d, 1)], buf_ref.at[kk, pl.ds(r, 1)], sem).start()
        return carry

    lax.fori_loop(0, tm, issue, 0)
    for kk in range(TOP_K):
        pltpu.make_async_copy(ys_ref.at[pl.ds(0, tm)], buf_ref.at[kk], sem).wait()
    gates = gate_ref[...]
    m = gates[:, 0:1] * buf_ref[0]
    for kk in range(1, TOP_K):
        m = m + gates[:, kk:kk + 1] * buf_ref[kk]
    o_ref[...] = _layernorm(alpha * x1_ref[...] + m, g2_ref[...], b2_ref[...])


def _combine(x1, ys, dest, rgate, ln_g3, ln_b3, layer, alpha):
    t, d = x1.shape
    tm = min(256, t)
    return pl.pallas_call(
        functools.partial(_combine_kernel, alpha=alpha),
        grid=(t // tm,),
        in_specs=[
            pl.BlockSpec((tm * TOP_K,), lambda i: (i,), memory_space=pltpu.SMEM),
            pl.BlockSpec((tm, LANES), lambda i: (i, 0)),
            pl.BlockSpec((tm, d), lambda i: (i, 0)),
            pl.BlockSpec(memory_space=pl.ANY),
            pl.BlockSpec((None, 1, d), lambda i: (layer, 0, 0)),
            pl.BlockSpec((None, 1, d), lambda i: (layer, 0, 0)),
        ],
        out_specs=pl.BlockSpec((tm, d), lambda i: (i, 0)),
        out_shape=jax.ShapeDtypeStruct((t, d), F32),
        scratch_shapes=[pltpu.VMEM((TOP_K, tm, d), F32), pltpu.SemaphoreType.DMA],
        compiler_params=_params("arbitrary"),
        name="moe_combine",
    )(dest, rgate, x1, ys, ln_g3, ln_b3)


def _block_diag_gate_weights(wa, wx):
    n_l, n_dir, n_h, hd, _ = wa.shape
    hh = n_h // 2
    eye = jnp.eye(hh, dtype=wa.dtype)

    def bd(wm):
        wm = wm.reshape(n_l, n_dir, 2, hh, hd, hd)
        full = jnp.einsum("ldghij,hk->ldghikj", wm, eye)
        return full.reshape(n_l, n_dir, 2, hh * hd, hh * hd)

    return jnp.concatenate([bd(wa), bd(wx)], axis=-1).astype(BF16)


def kernel(x, w_in, hg_lb, hg_norm, lru_conv_w, lru_conv_b, lru_wa, lru_ba, lru_wx, lru_bx, lru_lambda, lru_norm,
           w_out, ln1_g, ln1_b, router_w, router_b, w_gate_up, b_gate_up, w_down, b_down, ln2_g, ln2_b):
    bsz, s, d = x.shape
    depth = w_in.shape[0]
    hg_w = hg_lb.shape[-1]
    lru_w = lru_lambda.shape[-1]
    n_exp = router_w.shape[-1]
    de = w_down.shape[2]
    t = bsz * s
    alpha = float((2 * depth) ** 0.25)
    block_rows = MXU_DIM
    assert w_in.shape[-1] == 5 * hg_w + 2 * lru_w and hg_w == lru_w
    assert s % 256 == 0 or s < 256

    p = jax.nn.softmax(hg_lb.astype(F32), axis=0)
    lower_bounds = jnp.clip(jnp.cumsum(p, axis=0) - p[0:1], 0.0, 1.0 - 1e-6)
    w_in_b = w_in.astype(BF16)
    w_out_b = w_out.astype(BF16)
    w_gu_b = w_gate_up.astype(BF16)
    w_dn_b = w_down.astype(BF16)
    wg = _block_diag_gate_weights(lru_wa, lru_wx)
    gate_bias = jnp.stack([lru_ba, lru_bx], axis=2).astype(F32)
    row3 = lambda a: a.astype(F32).reshape(depth, 1, a.shape[-1])
    b_gu4 = b_gate_up.astype(F32).reshape(depth, n_exp, 1, 2 * de)
    b_dn4 = b_down.astype(F32).reshape(depth, n_exp, 1, d)

    x2 = x.reshape(t, d)
    for layer in range(depth):
        proj = _in_proj(x2, w_in_b, layer)
        proj3 = proj.reshape(bsz, s, proj.shape[-1])
        o_f, o_b = _hgrn2(proj3, lower_bounds, layer, hg_w)
        h_f, h_b = _griffin(proj3, lru_conv_w, row3(lru_conv_b), wg, gate_bias, lru_lambda, layer, lru_w,
                            col_block=5)
        x1, ridx, rgate = _mix(x2, o_f.reshape(t, hg_w), o_b.reshape(t, hg_w), h_f.reshape(t, lru_w),
                               h_b.reshape(t, lru_w), proj, w_out_b, row3(hg_norm), row3(lru_norm), row3(ln1_g),
                               row3(ln1_b), router_w, row3(router_b), layer, alpha, zg_block=4, zy_block=6)
        dest, counts, pstarts, blk_e, n_used, n_rows = _routing_tables(ridx, n_exp, block_rows)
        xs = _dispatch(x1, dest, counts, pstarts, n_rows, block_rows)
        ys = _moe_ffn(xs, blk_e, n_used, w_gu_b, b_gu4, w_dn_b, b_dn4, layer, block_rows)
        x2 = _combine(x1, ys, dest, rgate, row3(ln2_g), row3(ln2_b), layer, alpha)
    return x2.reshape(bsz, s, d)
```

```python
import functools

import jax
import jax.numpy as jnp
from jax import lax
from jax.experimental import pallas as pl
from jax.experimental.pallas import tpu as pltpu

F32 = jnp.float32
BF16 = jnp.bfloat16

HG_HEADS = 4
HG_CHUNK = 32
LB_FLOOR = 1e-30
LRU_HEADS = 8
LRU_C = 8.0
CONV_WIDTH = 4
TOP_K = 4
SWIGLU_LIMIT = 7.0
SWIGLU_ALPHA = 1.702
LN_EPS = 1e-5
RMS_EPS = 1e-6

LANES = 128
SUBLANES = 8
MXU_DIM = 256
VMEM_LIMIT_BYTES = 56 * 1024 * 1024

NT_DIMS = (((1,), (1,)), ((), ()))
TN_DIMS = (((0,), (0,)), ((), ()))


def _params(*semantics):
    return pltpu.CompilerParams(dimension_semantics=semantics, vmem_limit_bytes=VMEM_LIMIT_BYTES)


def _bdot(a, b):
    return jnp.dot(a.astype(BF16), b.astype(BF16), preferred_element_type=F32)


def _bdot_general(a, b, dims):
    return lax.dot_general(a.astype(BF16), b.astype(BF16), dims, preferred_element_type=F32)


def _layernorm(t, g, b):
    mu = jnp.mean(t, axis=-1, keepdims=True)
    c = t - mu
    var = jnp.mean(c * c, axis=-1, keepdims=True)
    return c * lax.rsqrt(var + LN_EPS) * g + b


def _in_proj_kernel(x_ref, w_ref, o_ref):
    o_ref[...] = jnp.dot(x_ref[...].astype(BF16), w_ref[...], preferred_element_type=F32)


def _in_proj(x2, w_in_bf16, layer):
    t, d = x2.shape
    n = w_in_bf16.shape[-1]
    tm = min(1024, t)
    tn = n // 2
    return pl.pallas_call(
        _in_proj_kernel,
        grid=(t // tm, n // tn),
        in_specs=[
            pl.BlockSpec((tm, d), lambda i, j: (i, 0)),
            pl.BlockSpec((None, d, tn), lambda i, j: (layer, 0, j)),
        ],
        out_specs=pl.BlockSpec((tm, tn), lambda i, j: (i, j)),
        out_shape=jax.ShapeDtypeStruct((t, n), F32),
        compiler_params=_params("parallel", "parallel"),
        name="in_proj",
    )(x2, w_in_bf16)


def _cumsum_rows(x):
    n = x.shape[0]
    row = lax.broadcasted_iota(jnp.int32, x.shape, 0)
    s = 1
    while s < n:
        x = x + jnp.where(row >= s, pltpu.roll(x, s, 0), 0.0)
        s *= 2
    return x


def _hg_chunk(zq, zv, zf, lb, log_lb, log1m_lb, st_ref, o_ref, r0, reverse):
    c = HG_CHUNK
    q = zq * jax.nn.sigmoid(zq)
    e = jnp.exp(-jnp.abs(zf))
    log_sig = jnp.minimum(zf, 0.0) - jnp.log1p(e)
    sig_neg = jnp.where(zf >= 0.0, e, 1.0) / (1.0 + e)
    b = log1m_lb + log_sig
    logf = jnp.maximum(log_lb, b) + jnp.log1p(jnp.exp(-jnp.abs(log_lb - b)))
    k = (1.0 - lb) * sig_neg

    p = _cumsum_rows(logf)
    g_last = p[c - 1:c]
    if reverse:
        g = g_last - p + logf
        g_ref = g[c // 2:c // 2 + 1]
    else:
        g = p
        g_ref = g[c // 2 - 1:c // 2]
    qg = q * jnp.exp(g - g_ref)
    kg = k * jnp.exp(g_ref - g)
    kl = kg * jnp.exp(g_last - g_ref)
    qe = qg * jnp.exp(g_ref)
    dec = jnp.exp(g_last)

    row = lax.broadcasted_iota(jnp.int32, (c, c), 0)
    col = lax.broadcasted_iota(jnp.int32, (c, c), 1)
    keep = (col >= row) if reverse else (col <= row)
    hd = zq.shape[1] // HG_HEADS
    for h in range(HG_HEADS):
        sl = slice(h * hd, (h + 1) * hd)
        scores = jnp.where(keep, _bdot_general(qg[:, sl], kg[:, sl], NT_DIMS), 0.0)
        st = st_ref[h]
        o = _bdot(scores, zv[:, sl]) + _bdot_general(qe[:, sl], st, NT_DIMS)
        o_ref[pl.ds(r0, c), sl] = o
        st_ref[h] = st * dec[:, sl] + _bdot_general(zv[:, sl], kl[:, sl], TN_DIMS)


def _hgrn2_kernel(fqif_ref, bqi_ref, bzb_ref, lb_ref, of_ref, ob_ref, stf_ref, stb_ref):
    @pl.when(pl.program_id(1) == 0)
    def _():
        stf_ref[...] = jnp.zeros_like(stf_ref)
        stb_ref[...] = jnp.zeros_like(stb_ref)

    w = lb_ref.shape[1]
    lb = lb_ref[...]
    log_lb = jnp.log(jnp.maximum(lb, LB_FLOOR))
    log1m_lb = jnp.log1p(-lb)
    n_chunks = of_ref.shape[0] // HG_CHUNK

    def body(ci, carry):
        r0 = pl.multiple_of(ci * HG_CHUNK, HG_CHUNK)
        rows = pl.ds(r0, HG_CHUNK)
        _hg_chunk(fqif_ref[rows, 0:w], fqif_ref[rows, w:2 * w], fqif_ref[rows, 2 * w:3 * w],
                  lb[0:1], log_lb[0:1], log1m_lb[0:1], stf_ref, of_ref, r0, False)
        rb = pl.multiple_of((n_chunks - 1 - ci) * HG_CHUNK, HG_CHUNK)
        rows_b = pl.ds(rb, HG_CHUNK)
        _hg_chunk(bqi_ref[rows_b, 0:w], bqi_ref[rows_b, w:2 * w], bzb_ref[rows_b, :],
                  lb[1:2], log_lb[1:2], log1m_lb[1:2], stb_ref, ob_ref, rb, True)
        return carry

    lax.fori_loop(0, n_chunks, body, 0)


def _hgrn2(proj3, lower_bounds, layer, hg_width):
    bsz, s, _ = proj3.shape
    w = hg_width
    ts = min(256, s)
    ns = s // ts
    hd = w // HG_HEADS
    out = jax.ShapeDtypeStruct((bsz, s, w), F32)
    return pl.pallas_call(
        _hgrn2_kernel,
        grid=(bsz, ns),
        in_specs=[
            pl.BlockSpec((None, ts, 3 * w), lambda b, j: (b, j, 0)),
            pl.BlockSpec((None, ts, 2 * w), lambda b, j: (b, ns - 1 - j, 0)),
            pl.BlockSpec((None, ts, w), lambda b, j: (b, ns - 1 - j, 3)),
            pl.BlockSpec((None, 2, w), lambda b, j: (layer, 0, 0)),
        ],
        out_specs=[
            pl.BlockSpec((None, ts, w), lambda b, j: (b, j, 0)),
            pl.BlockSpec((None, ts, w), lambda b, j: (b, ns - 1 - j, 0)),
        ],
        out_shape=[out, out],
        scratch_shapes=[pltpu.VMEM((HG_HEADS, hd, hd), F32), pltpu.VMEM((HG_HEADS, hd, hd), F32)],
        compiler_params=_params("parallel", "arbitrary"),
        name="hgrn2",
    )(proj3, proj3, proj3, lower_bounds)


def _lin_scan(a, u, carry, h_ref, reverse):
    n = a.shape[0]
    sub = lax.broadcasted_iota(jnp.int32, a.shape, 0) & (SUBLANES - 1)
    s = 1
    while s < SUBLANES:
        shift = (n - s) if reverse else s
        m = (sub < SUBLANES - s) if reverse else (sub >= s)
        a_sh = jnp.where(m, pltpu.roll(a, shift, 0), 1.0)
        u_sh = jnp.where(m, pltpu.roll(u, shift, 0), 0.0)
        u = u + a * u_sh
        a = a * a_sh
        s *= 2
    groups = range(n // SUBLANES)
    for g in (reversed(groups) if reverse else groups):
        rows = slice(g * SUBLANES, (g + 1) * SUBLANES)
        hg = u[rows] + a[rows] * carry
        h_ref[rows, :] = hg
        carry = hg[0:1] if reverse else hg[SUBLANES - 1:SUBLANES]
    return carry


def _griffin_dir(main_ref, prev_ref, next_ref, is_first, is_last, ext_ref, cw, cb, wg_ref, bias, sp,
                 carry_ref, h_ref, reverse):
    ts, w = main_ref.shape
    halo = SUBLANES
    ext_ref[0:halo, :] = jnp.where(is_first, 0.0, prev_ref[...])
    ext_ref[halo:halo + ts, :] = main_ref[...]
    ext_ref[halo + ts:2 * halo + ts, :] = jnp.where(is_last, 0.0, next_ref[...])
    xc = cb
    for j in range(CONV_WIDTH):
        xc = xc + cw[j:j + 1] * ext_ref[halo - 2 + j:halo - 2 + j + ts, :]

    half = w // 2
    pre = [_bdot(xc[:, i * half:(i + 1) * half], wg_ref[i]) for i in range(2)]
    r_pre = jnp.concatenate([pre[0][:, :half], pre[1][:, :half]], axis=1) + bias[0:1]
    i_pre = jnp.concatenate([pre[0][:, half:], pre[1][:, half:]], axis=1) + bias[1:2]
    r = jax.nn.sigmoid(r_pre)
    ig = jax.nn.sigmoid(i_pre)
    log_a = (-LRU_C) * r * sp
    a = jnp.exp(log_a)
    u = jnp.sqrt(jnp.maximum(1.0 - a * a, 0.0)) * (ig * xc)
    carry_ref[0:1, :] = _lin_scan(a, u, carry_ref[0:1, :], h_ref, reverse)


def _griffin_kernel(fm_ref, fp_ref, fn_ref, bm_ref, bp_ref, bn_ref, cw_ref, cb_ref, wg_ref, bias_ref, lam_ref,
                    hf_ref, hb_ref, ext_ref, cf_ref, cbk_ref):
    j = pl.program_id(1)
    ns = pl.num_programs(1)

    @pl.when(j == 0)
    def _():
        cf_ref[...] = jnp.zeros_like(cf_ref)
        cbk_ref[...] = jnp.zeros_like(cbk_ref)

    cw = cw_ref[...]
    cb = cb_ref[...]
    sp = jax.nn.softplus(-lam_ref[...])
    _griffin_dir(fm_ref, fp_ref, fn_ref, j == 0, j == ns - 1, ext_ref, cw, cb, wg_ref.at[0], bias_ref[0], sp[0:1],
                 cf_ref, hf_ref, False)
    _griffin_dir(bm_ref, bp_ref, bn_ref, j == ns - 1, j == 0, ext_ref, cw, cb, wg_ref.at[1], bias_ref[1], sp[1:2],
                 cbk_ref, hb_ref, True)


def _griffin(proj3, conv_w, conv_b3, wg, gate_bias, lam, layer, lru_width, col_block):
    bsz, s, _ = proj3.shape
    w = lru_width
    ts = min(256, s)
    ns = s // ts
    tb = ts // SUBLANES
    nb8 = s // SUBLANES
    out = jax.ShapeDtypeStruct((bsz, s, w), F32)
    halo = (None, SUBLANES, w)
    return pl.pallas_call(
        _griffin_kernel,
        grid=(bsz, ns),
        in_specs=[
            pl.BlockSpec((None, ts, w), lambda b, j: (b, j, col_block)),
            pl.BlockSpec(halo, lambda b, j: (b, jnp.maximum(j * tb - 1, 0), col_block)),
            pl.BlockSpec(halo, lambda b, j: (b, jnp.minimum((j + 1) * tb, nb8 - 1), col_block)),
            pl.BlockSpec((None, ts, w), lambda b, j: (b, ns - 1 - j, col_block)),
            pl.BlockSpec(halo, lambda b, j: (b, jnp.maximum((ns - 1 - j) * tb - 1, 0), col_block)),
            pl.BlockSpec(halo, lambda b, j: (b, jnp.minimum((ns - j) * tb, nb8 - 1), col_block)),
            pl.BlockSpec((None, CONV_WIDTH, w), lambda b, j: (layer, 0, 0)),
            pl.BlockSpec((None, 1, w), lambda b, j: (layer, 0, 0)),
            pl.BlockSpec((None, 2, 2, w // 2, w), lambda b, j: (layer, 0, 0, 0, 0)),
            pl.BlockSpec((None, 2, 2, w), lambda b, j: (layer, 0, 0, 0)),
            pl.BlockSpec((None, 2, w), lambda b, j: (layer, 0, 0)),
        ],
        out_specs=[
            pl.BlockSpec((None, ts, w), lambda b, j: (b, j, 0)),
            pl.BlockSpec((None, ts, w), lambda b, j: (b, ns - 1 - j, 0)),
        ],
        out_shape=[out, out],
        scratch_shapes=[
            pltpu.VMEM((ts + 2 * SUBLANES, w), F32),
            pltpu.VMEM((SUBLANES, w), F32),
            pltpu.VMEM((SUBLANES, w), F32),
        ],
        compiler_params=_params("parallel", "arbitrary"),
        name="griffin",
    )(proj3, proj3, proj3, proj3, proj3, proj3, conv_w, conv_b3, wg, gate_bias, lam)


def _gelu_tanh(x):
    return 0.5 * x * (1.0 + jnp.tanh(0.7978845608028654 * (x + 0.044715 * x * x * x)))


def _mix_kernel(x_ref, of_ref, ob_ref, zg_ref, hf_ref, hb_ref, zy_ref, wout_ref, hgn_ref, lrn_ref, g1_ref, b1_ref,
                rw_ref, rb_ref, x1_ref, ridx_ref, rgate_ref, *, alpha):
    o = of_ref[...] + ob_ref[...]
    w = o.shape[1]
    hd = w // HG_HEADS
    parts = []
    for h in range(HG_HEADS):
        oh = o[:, h * hd:(h + 1) * hd]
        parts.append(oh * lax.rsqrt(jnp.mean(oh * oh, axis=-1, keepdims=True) + RMS_EPS))
    zg = zg_ref[...]
    o_hg = jnp.concatenate(parts, axis=1) * hgn_ref[...] * (zg * jax.nn.sigmoid(zg))
    hh = hf_ref[...] + hb_ref[...]
    o_lru = hh * lax.rsqrt(jnp.mean(hh * hh, axis=-1, keepdims=True) + RMS_EPS) * lrn_ref[...]
    o_lru = o_lru * _gelu_tanh(zy_ref[...])
    y = _bdot(o_hg, wout_ref[0:w, :]) + _bdot(o_lru, wout_ref[w:, :])
    x1 = _layernorm(alpha * x_ref[...] + y, g1_ref[...], b1_ref[...])
    x1_ref[...] = x1

    rw = rw_ref[...]
    x_hi, w_hi = x1.astype(BF16), rw.astype(BF16)
    x_lo = (x1 - x_hi.astype(F32)).astype(BF16)
    w_lo = (rw - w_hi.astype(F32)).astype(BF16)
    logits = (jnp.dot(x_hi, w_hi, preferred_element_type=F32) + jnp.dot(x_lo, w_hi, preferred_element_type=F32)
              + jnp.dot(x_hi, w_lo, preferred_element_type=F32) + rb_ref[...])
    n_exp = logits.shape[1]
    eid = lax.broadcasted_iota(jnp.int32, logits.shape, 1)
    vals, idxs = [], []
    for _ in range(TOP_K):
        m = jnp.max(logits, axis=-1, keepdims=True)
        idx = jnp.min(jnp.where(logits == m, eid, n_exp), axis=-1, keepdims=True)
        vals.append(m)
        idxs.append(idx)
        logits = jnp.where(eid == idx, -jnp.inf, logits)
    exps = [jnp.exp(v - vals[0]) for v in vals]
    denom = exps[0] + exps[1] + exps[2] + exps[3]
    lane = lax.broadcasted_iota(jnp.int32, ridx_ref.shape, 1)
    ridx = jnp.zeros(ridx_ref.shape, jnp.int32)
    rgate = jnp.zeros(rgate_ref.shape, F32)
    for kk in range(TOP_K):
        ridx = jnp.where(lane == kk, idxs[kk], ridx)
        rgate = jnp.where(lane == kk, exps[kk] / denom, rgate)
    ridx_ref[...] = ridx
    rgate_ref[...] = rgate


def _mix(x2, o_f, o_b, h_f, h_b, proj, w_out_bf16, hg_norm3, lru_norm3, ln_g3, ln_b3, router_w, router_b3, layer,
         alpha, zg_block, zy_block):
    t, d = x2.shape
    w = o_f.shape[1]
    n_exp = router_w.shape[-1]
    tm = min(512, t)
    row_blk = lambda width: pl.BlockSpec((tm, width), lambda i: (i, 0))
    vec = lambda width: pl.BlockSpec((None, 1, width), lambda i: (layer, 0, 0))
    return pl.pallas_call(
        functools.partial(_mix_kernel, alpha=alpha),
        grid=(t // tm,),
        in_specs=[
            row_blk(d), row_blk(w), row_blk(w),
            pl.BlockSpec((tm, w), lambda i: (i, zg_block)),
            row_blk(w), row_blk(w),
            pl.BlockSpec((tm, w), lambda i: (i, zy_block)),
            pl.BlockSpec((None, 2 * w, d), lambda i: (layer, 0, 0)),
            vec(w), vec(w), vec(d), vec(d),
            pl.BlockSpec((None, d, n_exp), lambda i: (layer, 0, 0)),
            vec(n_exp),
        ],
        out_specs=[row_blk(d), row_blk(LANES), row_blk(LANES)],
        out_shape=[
            jax.ShapeDtypeStruct((t, d), F32),
            jax.ShapeDtypeStruct((t, LANES), jnp.int32),
            jax.ShapeDtypeStruct((t, LANES), F32),
        ],
        compiler_params=_params("parallel"),
        name="mix_ln_router",
    )(x2, o_f, o_b, proj, h_f, h_b, proj, w_out_bf16, hg_norm3, lru_norm3, ln_g3, ln_b3, router_w, router_b3)


MOE_X_BUFFERS = 2
MOE_Y_BUFFERS = 3


def _routing_tables(ridx, n_tok, n_exp, block_rows):
    n_assign = n_tok * TOP_K
    flat_e = ridx[:, :TOP_K].reshape(-1)
    skey = jnp.sort(flat_e * n_assign + jnp.arange(n_assign, dtype=jnp.int32))
    edges = jnp.arange(n_exp + 1, dtype=jnp.int32) * n_assign
    bounds = jnp.searchsorted(skey, edges, method="compare_all").astype(jnp.int32)
    starts, counts = bounds[:-1], bounds[1:] - bounds[:-1]
    padded = ((counts + block_rows - 1) // block_rows) * block_rows
    pends = jnp.cumsum(padded)
    pstarts = pends - padded
    n_rows = n_assign + n_exp * block_rows
    n_blocks = n_rows // block_rows
    blk_start = jnp.arange(n_blocks, dtype=jnp.int32) * block_rows
    blk_e = jnp.minimum(jnp.searchsorted(pends, blk_start, side="right", method="compare_all"),
                        n_exp - 1).astype(jnp.int32)
    n_used = (pends[-1] // block_rows).astype(jnp.int32).reshape(1)

    row = jnp.arange(n_rows, dtype=jnp.int32)
    e_row = jnp.repeat(blk_e, block_rows)
    rank = row - pstarts[e_row]
    valid = rank < counts[e_row]
    assign = skey[jnp.clip(starts[e_row] + rank, 0, n_assign - 1)] - e_row * n_assign
    tok = assign // TOP_K
    gsrc = jnp.where(valid, tok, 0).astype(jnp.int32)
    scratch_row = n_assign + row % block_rows
    sdst = jnp.where(valid, (assign % TOP_K) * n_tok + tok, scratch_row).astype(jnp.int32)
    sdst_prev = jnp.concatenate([scratch_row[:block_rows], sdst])
    return gsrc, sdst_prev, blk_e, n_used, n_blocks


def _moe_kernel(blk_e_ref, n_used_ref, gcur_ref, gnext_ref, sprev_ref, x1_ref, wgu_f32_ref, bgu_ref, wdn_f32_ref,
                bdn_ref, out_ref, xbuf, xb_ref, hid_ref, ybuf, wgu_ref, wdn_ref, gsem, ssem):
    i = pl.program_id(0)
    last = n_used_ref[0]
    bm = xbuf.shape[1]
    de = wdn_ref.shape[0]
    d_out = wdn_ref.shape[1]
    chunk = MXU_DIM

    @pl.when((i <= last) & ((i == 0) | (blk_e_ref[i] != blk_e_ref[jnp.maximum(i - 1, 0)])))
    def _():
        rows = LANES

        def cast_gu(j, carry):
            sl = pl.ds(pl.multiple_of(j * rows, rows), rows)
            wgu_ref[sl, :] = wgu_f32_ref[sl, :].astype(BF16)
            return carry

        def cast_dn(j, carry):
            sl = pl.ds(pl.multiple_of(j * rows, rows), rows)
            wdn_ref[sl, :] = wdn_f32_ref[sl, :].astype(BF16)
            return carry

        lax.fori_loop(0, wgu_ref.shape[0] // rows, cast_gu, 0)
        lax.fori_loop(0, wdn_ref.shape[0] // rows, cast_dn, 0)

    def gather_start(idx_ref, r, slot):
        pltpu.make_async_copy(x1_ref.at[pl.ds(idx_ref[r], 1)], xbuf.at[slot, pl.ds(r, 1)],
                              gsem.at[slot]).start(priority=r % 2)

    def gather_wait(slot):
        pltpu.make_async_copy(x1_ref.at[pl.ds(0, bm)], xbuf.at[slot], gsem.at[slot]).wait()

    def scatter_start(r, slot):
        pltpu.make_async_copy(ybuf.at[slot, pl.ds(r, 1)], out_ref.at[pl.ds(sprev_ref[r], 1)],
                              ssem.at[slot]).start(priority=r % 2)

    def scatter_wait(slot):
        pltpu.make_async_copy(ybuf.at[slot], out_ref.at[pl.ds(0, bm)], ssem.at[slot]).wait()

    @pl.when(i == 0)
    def _():
        ybuf[...] = jnp.zeros_like(ybuf)
        for r in range(bm):
            gather_start(gcur_ref, r, 0)

    @pl.when(i <= last)
    def _():
        slot = i % MOE_X_BUFFERS
        nslot = (i + 1) % MOE_X_BUFFERS
        yslot = i % MOE_Y_BUFFERS
        pslot = (i + MOE_Y_BUFFERS - 1) % MOE_Y_BUFFERS
        gather_wait(slot)

        @pl.when(i >= MOE_Y_BUFFERS - 1)
        def _():
            scatter_wait(yslot)

        xb_ref[...] = xbuf[slot].astype(BF16)
        xb = xb_ref[...]

        n_gu = de // chunk
        n_dn = d_out // chunk
        dma_queue = [functools.partial(gather_start, gnext_ref, r, nslot) for r in range(bm)]
        dma_queue += [functools.partial(scatter_start, r, pslot) for r in range(bm)]
        per_unit = -(-len(dma_queue) // (2 * n_gu + n_dn))

        def issue(n):
            for start in dma_queue[:n]:
                start()
            del dma_queue[:n]

        for c in range(n_gu):
            issue(2 * per_unit)
            cols = slice(c * chunk, (c + 1) * chunk)
            ucols = slice(de + c * chunk, de + (c + 1) * chunk)
            gate = jnp.dot(xb, wgu_ref[:, cols], preferred_element_type=F32) + bgu_ref[:, cols]
            up = jnp.dot(xb, wgu_ref[:, ucols], preferred_element_type=F32) + bgu_ref[:, ucols]
            gate = jnp.minimum(gate, SWIGLU_LIMIT)
            up = jnp.clip(up, -SWIGLU_LIMIT, SWIGLU_LIMIT)
            hid_ref[:, cols] = ((up + 1.0) * (gate * jax.nn.sigmoid(SWIGLU_ALPHA * gate))).astype(BF16)
        hid = hid_ref[...]
        for c in range(n_dn):
            issue(per_unit if c < n_dn - 1 else len(dma_queue))
            cols = slice(c * chunk, (c + 1) * chunk)
            ybuf[yslot, :, cols] = jnp.dot(hid, wdn_ref[:, cols], preferred_element_type=F32) + bdn_ref[:, cols]

    @pl.when(i == last)
    def _():
        gather_wait((last + 1) % MOE_X_BUFFERS)
        scatter_wait((last + MOE_Y_BUFFERS - 1) % MOE_Y_BUFFERS)
        scatter_wait((last + MOE_Y_BUFFERS - 2) % MOE_Y_BUFFERS)


def _moe(x1, gsrc, sdst_prev, blk_e, n_used, n_blocks, w_gate_up, b_gu4, w_down, b_dn4, layer, block_rows):
    t, d = x1.shape
    de = w_down.shape[2]
    assert de % MXU_DIM == 0 and d % MXU_DIM == 0 and block_rows % (de // MXU_DIM) == 0
    assert t * TOP_K // block_rows >= MOE_Y_BUFFERS
    idx_blk = lambda fn: pl.BlockSpec((block_rows,), fn, memory_space=pltpu.SMEM)
    w_map = lambda i, be, nu: (layer, be[i], 0, 0)
    return pl.pallas_call(
        _moe_kernel,
        grid_spec=pltpu.PrefetchScalarGridSpec(
            num_scalar_prefetch=2,
            grid=(n_blocks,),
            in_specs=[
                idx_blk(lambda i, be, nu: (0,)),
                idx_blk(lambda i, be, nu: (jnp.minimum(i + 1, n_blocks - 1),)),
                idx_blk(lambda i, be, nu: (i,)),
                pl.BlockSpec(memory_space=pl.ANY),
                pl.BlockSpec((None, None, d, 2 * de), w_map),
                pl.BlockSpec((None, None, 1, 2 * de), w_map),
                pl.BlockSpec((None, None, de, d), w_map),
                pl.BlockSpec((None, None, 1, d), w_map),
            ],
            out_specs=pl.BlockSpec(memory_space=pl.ANY),
            scratch_shapes=[
                pltpu.VMEM((MOE_X_BUFFERS, block_rows, d), F32),
                pltpu.VMEM((block_rows, d), BF16),
                pltpu.VMEM((block_rows, de), BF16),
                pltpu.VMEM((MOE_Y_BUFFERS, block_rows, d), F32),
                pltpu.VMEM((d, 2 * de), BF16),
                pltpu.VMEM((de, d), BF16),
                pltpu.SemaphoreType.DMA((MOE_X_BUFFERS,)),
                pltpu.SemaphoreType.DMA((MOE_Y_BUFFERS,)),
            ],
        ),
        out_shape=jax.ShapeDtypeStruct((TOP_K * t + block_rows, d), F32),
        compiler_params=_params("arbitrary"),
        name="moe_ffn",
    )(blk_e, n_used, gsrc, gsrc, sdst_prev, x1, w_gate_up, b_gu4, w_down, b_dn4)


def _combine_kernel(gate_ref, x1_ref, y0_ref, y1_ref, y2_ref, y3_ref, g2_ref, b2_ref, o_ref, *, alpha):
    gates = gate_ref[...]
    m = gates[:, 0:1] * y0_ref[...]
    for kk, y_ref in enumerate((y1_ref, y2_ref, y3_ref), start=1):
        m = m + gates[:, kk:kk + 1] * y_ref[...]
    o_ref[...] = _layernorm(alpha * x1_ref[...] + m, g2_ref[...], b2_ref[...])


def _combine(x1, ys, rgate, ln_g3, ln_b3, layer, alpha):
    t, d = x1.shape
    tm = min(512, t)
    nt = t // tm
    plane = lambda kk: pl.BlockSpec((tm, d), lambda i: (kk * nt + i, 0))
    return pl.pallas_call(
        functools.partial(_combine_kernel, alpha=alpha),
        grid=(nt,),
        in_specs=[
            pl.BlockSpec((tm, LANES), lambda i: (i, 0)),
            pl.BlockSpec((tm, d), lambda i: (i, 0)),
            plane(0), plane(1), plane(2), plane(3),
            pl.BlockSpec((None, 1, d), lambda i: (layer, 0, 0)),
            pl.BlockSpec((None, 1, d), lambda i: (layer, 0, 0)),
        ],
        out_specs=pl.BlockSpec((tm, d), lambda i: (i, 0)),
        out_shape=jax.ShapeDtypeStruct((t, d), F32),
        compiler_params=_params("parallel"),
        name="moe_combine",
    )(rgate, x1, ys, ys, ys, ys, ln_g3, ln_b3)


def _block_diag_gate_weights(wa, wx):
    n_l, n_dir, n_h, hd, _ = wa.shape
    hh = n_h // 2
    eye = jnp.eye(hh, dtype=wa.dtype)

    def bd(wm):
        wm = wm.reshape(n_l, n_dir, 2, hh, hd, hd)
        full = jnp.einsum("ldghij,hk->ldghikj", wm, eye)
        return full.reshape(n_l, n_dir, 2, hh * hd, hh * hd)

    return jnp.concatenate([bd(wa), bd(wx)], axis=-1).astype(BF16)


def kernel(x, w_in, hg_lb, hg_norm, lru_conv_w, lru_conv_b, lru_wa, lru_ba, lru_wx, lru_bx, lru_lambda, lru_norm,
           w_out, ln1_g, ln1_b, router_w, router_b, w_gate_up, b_gate_up, w_down, b_down, ln2_g, ln2_b):
    bsz, s, d = x.shape
    depth = w_in.shape[0]
    hg_w = hg_lb.shape[-1]
    lru_w = lru_lambda.shape[-1]
    n_exp = router_w.shape[-1]
    de = w_down.shape[2]
    t = bsz * s
    alpha = float((2 * depth) ** 0.25)
    block_rows = MXU_DIM
    assert w_in.shape[-1] == 5 * hg_w + 2 * lru_w and hg_w == lru_w
    assert s % 256 == 0 or s < 256

    p = jax.nn.softmax(hg_lb.astype(F32), axis=0)
    lower_bounds = jnp.clip(jnp.cumsum(p, axis=0) - p[0:1], 0.0, 1.0 - 1e-6)
    w_in_b = w_in.astype(BF16)
    w_out_b = w_out.astype(BF16)
    wg = _block_diag_gate_weights(lru_wa, lru_wx)
    gate_bias = jnp.stack([lru_ba, lru_bx], axis=2).astype(F32)
    row3 = lambda a: a.astype(F32).reshape(depth, 1, a.shape[-1])
    b_gu4 = b_gate_up.astype(F32).reshape(depth, n_exp, 1, 2 * de)
    b_dn4 = b_down.astype(F32).reshape(depth, n_exp, 1, d)

    x2 = x.reshape(t, d)
    for layer in range(depth):
        proj = _in_proj(x2, w_in_b, layer)
        proj3 = proj.reshape(bsz, s, proj.shape[-1])
        o_f, o_b = _hgrn2(proj3, lower_bounds, layer, hg_w)
        h_f, h_b = _griffin(proj3, lru_conv_w, row3(lru_conv_b), wg, gate_bias, lru_lambda, layer, lru_w,
                            col_block=5)
        x1, ridx, rgate = _mix(x2, o_f.reshape(t, hg_w), o_b.reshape(t, hg_w), h_f.reshape(t, lru_w),
                               h_b.reshape(t, lru_w), proj, w_out_b, row3(hg_norm), row3(lru_norm), row3(ln1_g),
                               row3(ln1_b), router_w, row3(router_b), layer, alpha, zg_block=4, zy_block=6)
        gsrc, sdst_prev, blk_e, n_used, n_blocks = _routing_tables(ridx, t, n_exp, block_rows)
        ys = _moe(x1, gsrc, sdst_prev, blk_e, n_used, n_blocks, w_gate_up, b_gu4, w_down, b_dn4, layer, block_rows)
        x2 = _combine(x1, ys, rgate, row3(ln2_g), row3(ln2_b), layer, alpha)
    return x2.reshape(bsz, s, d)
```

```python
import functools

import jax
import jax.numpy as jnp
from jax import lax
from jax.experimental import pallas as pl
from jax.experimental.pallas import tpu as pltpu

F32 = jnp.float32
BF16 = jnp.bfloat16

HG_HEADS = 4
HG_CHUNK = 32
LB_FLOOR = 1e-30
LRU_HEADS = 8
LRU_C = 8.0
CONV_WIDTH = 4
TOP_K = 4
SWIGLU_LIMIT = 7.0
SWIGLU_ALPHA = 1.702
LN_EPS = 1e-5
RMS_EPS = 1e-6

LANES = 128
SUBLANES = 8
MXU_DIM = 256
VMEM_LIMIT_BYTES = 56 * 1024 * 1024

NT_DIMS = (((1,), (1,)), ((), ()))
TN_DIMS = (((0,), (0,)), ((), ()))


def _params(*semantics):
    return pltpu.CompilerParams(dimension_semantics=semantics, vmem_limit_bytes=VMEM_LIMIT_BYTES)


def _bdot(a, b):
    return jnp.dot(a.astype(BF16), b.astype(BF16), preferred_element_type=F32)


def _bdot_general(a, b, dims):
    return lax.dot_general(a.astype(BF16), b.astype(BF16), dims, preferred_element_type=F32)


def _layernorm(t, g, b):
    mu = jnp.mean(t, axis=-1, keepdims=True)
    c = t - mu
    var = jnp.mean(c * c, axis=-1, keepdims=True)
    return c * lax.rsqrt(var + LN_EPS) * g + b


def _in_proj_kernel(x_ref, w_ref, o_ref):
    o_ref[...] = jnp.dot(x_ref[...].astype(BF16), w_ref[...], preferred_element_type=F32)


def _in_proj(x2, w_in_bf16, layer):
    t, d = x2.shape
    n = w_in_bf16.shape[-1]
    tm = min(512, t)
    return pl.pallas_call(
        _in_proj_kernel,
        grid=(t // tm,),
        in_specs=[
            pl.BlockSpec((tm, d), lambda i: (i, 0)),
            pl.BlockSpec((None, d, n), lambda i: (layer, 0, 0)),
        ],
        out_specs=pl.BlockSpec((tm, n), lambda i: (i, 0)),
        out_shape=jax.ShapeDtypeStruct((t, n), F32),
        compiler_params=_params("parallel"),
        name="in_proj",
    )(x2, w_in_bf16)


def _cumsum_rows(x):
    n = x.shape[0]
    row = lax.broadcasted_iota(jnp.int32, x.shape, 0)
    s = 1
    while s < n:
        x = x + jnp.where(row >= s, pltpu.roll(x, s, 0), 0.0)
        s *= 2
    return x


def _hg_chunk(zq, zv, zf, lb, log_lb, log1m_lb, st_ref, o_ref, r0, reverse):
    c = HG_CHUNK
    q = zq * jax.nn.sigmoid(zq)
    e = jnp.exp(-jnp.abs(zf))
    log_sig = jnp.minimum(zf, 0.0) - jnp.log1p(e)
    sig_neg = jnp.where(zf >= 0.0, e, 1.0) / (1.0 + e)
    b = log1m_lb + log_sig
    logf = jnp.maximum(log_lb, b) + jnp.log1p(jnp.exp(-jnp.abs(log_lb - b)))
    k = (1.0 - lb) * sig_neg

    p = _cumsum_rows(logf)
    g_last = p[c - 1:c]
    if reverse:
        g = g_last - p + logf
        g_ref = g[c // 2:c // 2 + 1]
    else:
        g = p
        g_ref = g[c // 2 - 1:c // 2]
    qg = q * jnp.exp(g - g_ref)
    kg = k * jnp.exp(g_ref - g)
    kl = kg * jnp.exp(g_last - g_ref)
    qe = qg * jnp.exp(g_ref)
    dec = jnp.exp(g_last)

    row = lax.broadcasted_iota(jnp.int32, (c, c), 0)
    col = lax.broadcasted_iota(jnp.int32, (c, c), 1)
    keep = (col >= row) if reverse else (col <= row)
    hd = zq.shape[1] // HG_HEADS
    for h in range(HG_HEADS):
        sl = slice(h * hd, (h + 1) * hd)
        scores = jnp.where(keep, _bdot_general(qg[:, sl], kg[:, sl], NT_DIMS), 0.0)
        st = st_ref[h]
        o = _bdot(scores, zv[:, sl]) + _bdot_general(qe[:, sl], st, NT_DIMS)
        o_ref[pl.ds(r0, c), sl] = o
        st_ref[h] = st * dec[:, sl] + _bdot_general(zv[:, sl], kl[:, sl], TN_DIMS)


def _hgrn2_kernel(fqif_ref, bqi_ref, bzb_ref, lb_ref, of_ref, ob_ref, stf_ref, stb_ref):
    @pl.when(pl.program_id(1) == 0)
    def _():
        stf_ref[...] = jnp.zeros_like(stf_ref)
        stb_ref[...] = jnp.zeros_like(stb_ref)

    w = lb_ref.shape[1]
    lb = lb_ref[...]
    log_lb = jnp.log(jnp.maximum(lb, LB_FLOOR))
    log1m_lb = jnp.log1p(-lb)
    n_chunks = of_ref.shape[0] // HG_CHUNK

    def body(ci, carry):
        r0 = pl.multiple_of(ci * HG_CHUNK, HG_CHUNK)
        rows = pl.ds(r0, HG_CHUNK)
        _hg_chunk(fqif_ref[rows, 0:w], fqif_ref[rows, w:2 * w], fqif_ref[rows, 2 * w:3 * w],
                  lb[0:1], log_lb[0:1], log1m_lb[0:1], stf_ref, of_ref, r0, False)
        rb = pl.multiple_of((n_chunks - 1 - ci) * HG_CHUNK, HG_CHUNK)
        rows_b = pl.ds(rb, HG_CHUNK)
        _hg_chunk(bqi_ref[rows_b, 0:w], bqi_ref[rows_b, w:2 * w], bzb_ref[rows_b, :],
                  lb[1:2], log_lb[1:2], log1m_lb[1:2], stb_ref, ob_ref, rb, True)
        return carry

    lax.fori_loop(0, n_chunks, body, 0, unroll=2)


def _hgrn2(proj3, lower_bounds, layer, hg_width):
    bsz, s, _ = proj3.shape
    w = hg_width
    ts = min(256, s)
    ns = s // ts
    hd = w // HG_HEADS
    out = jax.ShapeDtypeStruct((bsz, s, w), F32)
    return pl.pallas_call(
        _hgrn2_kernel,
        grid=(bsz, ns),
        in_specs=[
            pl.BlockSpec((None, ts, 3 * w), lambda b, j: (b, j, 0)),
            pl.BlockSpec((None, ts, 2 * w), lambda b, j: (b, ns - 1 - j, 0)),
            pl.BlockSpec((None, ts, w), lambda b, j: (b, ns - 1 - j, 3)),
            pl.BlockSpec((None, 2, w), lambda b, j: (layer, 0, 0)),
        ],
        out_specs=[
            pl.BlockSpec((None, ts, w), lambda b, j: (b, j, 0)),
            pl.BlockSpec((None, ts, w), lambda b, j: (b, ns - 1 - j, 0)),
        ],
        out_shape=[out, out],
        scratch_shapes=[pltpu.VMEM((HG_HEADS, hd, hd), F32), pltpu.VMEM((HG_HEADS, hd, hd), F32)],
        compiler_params=_params("parallel", "arbitrary"),
        name="hgrn2",
    )(proj3, proj3, proj3, lower_bounds)


def _lin_scan(a, u, carry, h_ref, reverse):
    n, w = a.shape
    n_groups = n // SUBLANES
    a = a.reshape(n_groups, SUBLANES, w)
    u = u.reshape(n_groups, SUBLANES, w)
    sub = lax.broadcasted_iota(jnp.int32, a.shape, 1)
    s = 1
    while s < SUBLANES:
        shift = (SUBLANES - s) if reverse else s
        m = (sub < SUBLANES - s) if reverse else (sub >= s)
        a_sh = jnp.where(m, pltpu.roll(a, shift, 1), 1.0)
        u_sh = jnp.where(m, pltpu.roll(u, shift, 1), 0.0)
        u = u + a * u_sh
        a = a * a_sh
        s *= 2
    groups = range(n_groups)
    for g in (reversed(groups) if reverse else groups):
        hg = u[g] + a[g] * carry
        h_ref[g * SUBLANES:(g + 1) * SUBLANES, :] = hg
        carry = hg[0:1] if reverse else hg[SUBLANES - 1:SUBLANES]
    return carry


def _griffin_dir(main_ref, prev_ref, next_ref, is_first, is_last, ext_ref, cw, cb, wg_ref, bias, sp,
                 carry_ref, h_ref, reverse):
    ts, w = main_ref.shape
    halo = SUBLANES
    ext_ref[0:halo, :] = jnp.where(is_first, 0.0, prev_ref[...])
    ext_ref[halo:halo + ts, :] = main_ref[...]
    ext_ref[halo + ts:2 * halo + ts, :] = jnp.where(is_last, 0.0, next_ref[...])
    xc = cb
    for j in range(CONV_WIDTH):
        xc = xc + cw[j:j + 1] * ext_ref[halo - 2 + j:halo - 2 + j + ts, :]

    half = w // 2
    pre = [_bdot(xc[:, i * half:(i + 1) * half], wg_ref[i]) for i in range(2)]
    r_pre = jnp.concatenate([pre[0][:, :half], pre[1][:, :half]], axis=1) + bias[0:1]
    i_pre = jnp.concatenate([pre[0][:, half:], pre[1][:, half:]], axis=1) + bias[1:2]
    r = jax.nn.sigmoid(r_pre)
    ig = jax.nn.sigmoid(i_pre)
    log_a = (-LRU_C) * r * sp
    a = jnp.exp(log_a)
    u = jnp.sqrt(jnp.maximum(1.0 - a * a, 0.0)) * (ig * xc)
    carry_ref[0:1, :] = _lin_scan(a, u, carry_ref[0:1, :], h_ref, reverse)


def _griffin_kernel(fm_ref, fp_ref, fn_ref, bm_ref, bp_ref, bn_ref, cw_ref, cb_ref, wg_ref, bias_ref, lam_ref,
                    hf_ref, hb_ref, ext_ref, cf_ref, cbk_ref):
    j = pl.program_id(1)
    ns = pl.num_programs(1)

    @pl.when(j == 0)
    def _():
        cf_ref[...] = jnp.zeros_like(cf_ref)
        cbk_ref[...] = jnp.zeros_like(cbk_ref)

    cw = cw_ref[...]
    cb = cb_ref[...]
    sp = jax.nn.softplus(-lam_ref[...])
    _griffin_dir(fm_ref, fp_ref, fn_ref, j == 0, j == ns - 1, ext_ref, cw, cb, wg_ref.at[0], bias_ref[0], sp[0:1],
                 cf_ref, hf_ref, False)
    _griffin_dir(bm_ref, bp_ref, bn_ref, j == ns - 1, j == 0, ext_ref, cw, cb, wg_ref.at[1], bias_ref[1], sp[1:2],
                 cbk_ref, hb_ref, True)


def _griffin(proj3, conv_w, conv_b3, wg, gate_bias, lam, layer, lru_width, col_block):
    bsz, s, _ = proj3.shape
    w = lru_width
    ts = min(256, s)
    ns = s // ts
    tb = ts // SUBLANES
    nb8 = s // SUBLANES
    out = jax.ShapeDtypeStruct((bsz, s, w), F32)
    halo = (None, SUBLANES, w)
    return pl.pallas_call(
        _griffin_kernel,
        grid=(bsz, ns),
        in_specs=[
            pl.BlockSpec((None, ts, w), lambda b, j: (b, j, col_block)),
            pl.BlockSpec(halo, lambda b, j: (b, jnp.maximum(j * tb - 1, 0), col_block)),
            pl.BlockSpec(halo, lambda b, j: (b, jnp.minimum((j + 1) * tb, nb8 - 1), col_block)),
            pl.BlockSpec((None, ts, w), lambda b, j: (b, ns - 1 - j, col_block)),
            pl.BlockSpec(halo, lambda b, j: (b, jnp.maximum((ns - 1 - j) * tb - 1, 0), col_block)),
            pl.BlockSpec(halo, lambda b, j: (b, jnp.minimum((ns - j) * tb, nb8 - 1), col_block)),
            pl.BlockSpec((None, CONV_WIDTH, w), lambda b, j: (layer, 0, 0)),
            pl.BlockSpec((None, 1, w), lambda b, j: (layer, 0, 0)),
            pl.BlockSpec((None, 2, 2, w // 2, w), lambda b, j: (layer, 0, 0, 0, 0)),
            pl.BlockSpec((None, 2, 2, w), lambda b, j: (layer, 0, 0, 0)),
            pl.BlockSpec((None, 2, w), lambda b, j: (layer, 0, 0)),
        ],
        out_specs=[
            pl.BlockSpec((None, ts, w), lambda b, j: (b, j, 0)),
            pl.BlockSpec((None, ts, w), lambda b, j: (b, ns - 1 - j, 0)),
        ],
        out_shape=[out, out],
        scratch_shapes=[
            pltpu.VMEM((ts + 2 * SUBLANES, w), F32),
            pltpu.VMEM((SUBLANES, w), F32),
            pltpu.VMEM((SUBLANES, w), F32),
        ],
        compiler_params=_params("parallel", "arbitrary"),
        name="griffin",
    )(proj3, proj3, proj3, proj3, proj3, proj3, conv_w, conv_b3, wg, gate_bias, lam)


def _gelu_tanh(x):
    return 0.5 * x * (1.0 + jnp.tanh(0.7978845608028654 * (x + 0.044715 * x * x * x)))


def _mix_kernel(x_ref, of_ref, ob_ref, zg_ref, hf_ref, hb_ref, zy_ref, wout_ref, hgn_ref, lrn_ref, g1_ref, b1_ref,
                rw_ref, rb_ref, tri_ref, x1_ref, ridx_ref, rgate_ref, counts_ref, cnt_ref, *, alpha):
    @pl.when(pl.program_id(0) == 0)
    def _():
        cnt_ref[...] = jnp.zeros_like(cnt_ref)

    o = of_ref[...] + ob_ref[...]
    w = o.shape[1]
    hd = w // HG_HEADS
    parts = []
    for h in range(HG_HEADS):
        oh = o[:, h * hd:(h + 1) * hd]
        parts.append(oh * lax.rsqrt(jnp.mean(oh * oh, axis=-1, keepdims=True) + RMS_EPS))
    zg = zg_ref[...]
    o_hg = jnp.concatenate(parts, axis=1) * hgn_ref[...] * (zg * jax.nn.sigmoid(zg))
    hh = hf_ref[...] + hb_ref[...]
    o_lru = hh * lax.rsqrt(jnp.mean(hh * hh, axis=-1, keepdims=True) + RMS_EPS) * lrn_ref[...]
    o_lru = o_lru * _gelu_tanh(zy_ref[...])
    y = _bdot(o_hg, wout_ref[0:w, :]) + _bdot(o_lru, wout_ref[w:, :])
    x1 = _layernorm(alpha * x_ref[...] + y, g1_ref[...], b1_ref[...])
    x1_ref[...] = x1

    rw = rw_ref[...]
    x_hi, w_hi = x1.astype(BF16), rw.astype(BF16)
    x_lo = (x1 - x_hi.astype(F32)).astype(BF16)
    w_lo = (rw - w_hi.astype(F32)).astype(BF16)
    logits = (jnp.dot(x_hi, w_hi, preferred_element_type=F32) + jnp.dot(x_lo, w_hi, preferred_element_type=F32)
              + jnp.dot(x_hi, w_lo, preferred_element_type=F32) + rb_ref[...])
    n_exp = logits.shape[1]
    eid = lax.broadcasted_iota(jnp.int32, logits.shape, 1)
    vals, idxs = [], []
    for _ in range(TOP_K):
        m = jnp.max(logits, axis=-1, keepdims=True)
        idx = jnp.min(jnp.where(logits == m, eid, n_exp), axis=-1, keepdims=True)
        vals.append(m)
        idxs.append(idx)
        logits = jnp.where(eid == idx, -jnp.inf, logits)
    exps = [jnp.exp(v - vals[0]) for v in vals]
    denom = exps[0] + exps[1] + exps[2] + exps[3]

    base = cnt_ref[0:1, 0:n_exp]
    ranks = []
    for kk in range(TOP_K):
        onehot = (eid == idxs[kk]).astype(F32)
        before = jnp.dot(tri_ref[...], onehot.astype(BF16), preferred_element_type=F32) + base
        ranks.append(jnp.sum(onehot * before, axis=-1, keepdims=True).astype(jnp.int32))
        base = base + jnp.sum(onehot, axis=0, keepdims=True)
    cnt_ref[0:1, 0:n_exp] = base
    counts_ref[...] = cnt_ref[...]

    lane = lax.broadcasted_iota(jnp.int32, ridx_ref.shape, 1)
    ridx = jnp.zeros(ridx_ref.shape, jnp.int32)
    rgate = jnp.zeros(rgate_ref.shape, F32)
    for kk in range(TOP_K):
        ridx = jnp.where(lane == kk, idxs[kk], ridx)
        ridx = jnp.where(lane == TOP_K + kk, ranks[kk], ridx)
        rgate = jnp.where(lane == kk, exps[kk] / denom, rgate)
    ridx_ref[...] = ridx
    rgate_ref[...] = rgate


def _mix(x2, o_f, o_b, h_f, h_b, proj, w_out_bf16, hg_norm3, lru_norm3, ln_g3, ln_b3, router_w, router_b3, layer,
         alpha, zg_block, zy_block):
    t, d = x2.shape
    w = o_f.shape[1]
    n_exp = router_w.shape[-1]
    tm = min(512, t)
    row_blk = lambda width: pl.BlockSpec((tm, width), lambda i: (i, 0))
    vec = lambda width: pl.BlockSpec((None, 1, width), lambda i: (layer, 0, 0))
    return pl.pallas_call(
        functools.partial(_mix_kernel, alpha=alpha),
        grid=(t // tm,),
        in_specs=[
            row_blk(d), row_blk(w), row_blk(w),
            pl.BlockSpec((tm, w), lambda i: (i, zg_block)),
            row_blk(w), row_blk(w),
            pl.BlockSpec((tm, w), lambda i: (i, zy_block)),
            pl.BlockSpec((None, 2 * w, d), lambda i: (layer, 0, 0)),
            vec(w), vec(w), vec(d), vec(d),
            pl.BlockSpec((None, d, n_exp), lambda i: (layer, 0, 0)),
            vec(n_exp),
            pl.BlockSpec((tm, tm), lambda i: (0, 0)),
        ],
        out_specs=[row_blk(d), row_blk(LANES), row_blk(LANES), pl.BlockSpec((SUBLANES, LANES), lambda i: (0, 0))],
        out_shape=[
            jax.ShapeDtypeStruct((t, d), F32),
            jax.ShapeDtypeStruct((t, LANES), jnp.int32),
            jax.ShapeDtypeStruct((t, LANES), F32),
            jax.ShapeDtypeStruct((SUBLANES, LANES), F32),
        ],
        scratch_shapes=[pltpu.VMEM((SUBLANES, LANES), F32)],
        compiler_params=_params("arbitrary"),
        name="mix_ln_router",
    )(x2, o_f, o_b, proj, h_f, h_b, proj, w_out_bf16, hg_norm3, lru_norm3, ln_g3, ln_b3, router_w, router_b3,
      jnp.tril(jnp.ones((tm, tm), BF16), k=-1))


DMA_UNROLL_ROWS = 32


def _routing_tables(ridx, counts_f32, n_exp, block_rows):
    t = ridx.shape[0]
    idx = ridx[:, :TOP_K]
    rank = ridx[:, TOP_K:2 * TOP_K]
    counts = counts_f32[0, :n_exp].astype(jnp.int32)
    padded = ((counts + block_rows - 1) // block_rows) * block_rows
    pends = jnp.cumsum(padded)
    pstarts = pends - padded
    onehot = idx[:, :, None] == jnp.arange(n_exp, dtype=jnp.int32)[None, None, :]
    dest = (jnp.sum(jnp.where(onehot, pstarts[None, None, :], 0), axis=-1) + rank).reshape(-1).astype(jnp.int32)
    n_rows = t * TOP_K + n_exp * block_rows
    n_blocks = n_rows // block_rows
    blk_start = jnp.arange(n_blocks, dtype=jnp.int32) * block_rows
    blk_e = jnp.minimum(jnp.searchsorted(pends, blk_start, side="right", method="compare_all"),
                        n_exp - 1).astype(jnp.int32)
    n_used = (pends[-1] // block_rows).astype(jnp.int32).reshape(1)
    return dest, counts, pstarts.astype(jnp.int32), blk_e, n_used, n_rows


def _dispatch_kernel(cnt_ref, pst_ref, dest_ref, x_ref, xs_ref, zero_ref, sem, *, block_rows):
    tm = x_ref.shape[0]

    def issue(g, carry):
        r0 = pl.multiple_of(g * DMA_UNROLL_ROWS, DMA_UNROLL_ROWS)
        for j in range(DMA_UNROLL_ROWS):
            for kk in range(TOP_K):
                d = dest_ref[(r0 + j) * TOP_K + kk]
                pltpu.make_async_copy(x_ref.at[pl.ds(r0 + j, 1)], xs_ref.at[pl.ds(d, 1)], sem).start(priority=kk % 2)
        return carry

    lax.fori_loop(0, tm // DMA_UNROLL_ROWS, issue, 0)
    for _ in range(TOP_K):
        pltpu.make_async_copy(x_ref, xs_ref.at[pl.ds(0, tm)], sem).wait()

    @pl.when(pl.program_id(0) == pl.num_programs(0) - 1)
    def _():
        zero_ref[...] = jnp.zeros_like(zero_ref)
        zrow = zero_ref.at[pl.ds(0, 1)]
        n_exp = cnt_ref.shape[0]
        last = n_exp - 1
        cnt_last = cnt_ref[last]
        used_rows = pst_ref[last] + cnt_last + (block_rows - cnt_last % block_rows) % block_rows
        n_tail = (xs_ref.shape[0] - used_rows) // block_rows

        def tail_copy(b):
            start = pl.multiple_of(used_rows + b * block_rows, block_rows)
            return pltpu.make_async_copy(zero_ref, xs_ref.at[pl.ds(start, block_rows)], sem)

        def tail_start(b, c2):
            tail_copy(b).start()
            return c2

        def tail_wait(b, c2):
            tail_copy(b).wait()
            return c2

        lax.fori_loop(0, n_tail, tail_start, 0)
        lax.fori_loop(0, n_tail, tail_wait, 0)

        def per_expert(e, carry):
            cnt = cnt_ref[e]
            first = pst_ref[e] + cnt
            n_pad = (block_rows - cnt % block_rows) % block_rows

            def pad_start(r, c2):
                pltpu.make_async_copy(zrow, xs_ref.at[pl.ds(first + r, 1)], sem).start()
                return c2

            def pad_wait(r, c2):
                pltpu.make_async_copy(zrow, xs_ref.at[pl.ds(first + r, 1)], sem).wait()
                return c2

            lax.fori_loop(0, n_pad, pad_start, 0)
            lax.fori_loop(0, n_pad, pad_wait, 0)
            return carry

        lax.fori_loop(0, n_exp, per_expert, 0)


def _dispatch(x1, dest, counts, pstarts, n_rows, block_rows):
    t, d = x1.shape
    tm = min(512, t)
    assert tm % DMA_UNROLL_ROWS == 0
    return pl.pallas_call(
        functools.partial(_dispatch_kernel, block_rows=block_rows),
        grid_spec=pltpu.PrefetchScalarGridSpec(
            num_scalar_prefetch=2,
            grid=(t // tm,),
            in_specs=[
                pl.BlockSpec((tm * TOP_K,), lambda i, c, p: (i,), memory_space=pltpu.SMEM),
                pl.BlockSpec((tm, d), lambda i, c, p: (i, 0)),
            ],
            out_specs=pl.BlockSpec(memory_space=pl.ANY),
            scratch_shapes=[pltpu.VMEM((block_rows, d), F32), pltpu.SemaphoreType.DMA],
        ),
        out_shape=jax.ShapeDtypeStruct((n_rows, d), F32),
        compiler_params=_params("arbitrary"),
        name="moe_dispatch",
    )(counts, pstarts, dest, x1)


def _moe_ffn_kernel(blk_e_ref, n_used_ref, xs_ref, wgu_f32_ref, bgu_ref, wdn_f32_ref, bdn_ref, ys_ref,
                    wgu_ref, wdn_ref):
    i = pl.program_id(0)
    used = i < n_used_ref[0]

    @pl.when(used & ((i == 0) | (blk_e_ref[i] != blk_e_ref[jnp.maximum(i - 1, 0)])))
    def _():
        rows = LANES

        def cast_gu(j, carry):
            sl = pl.ds(pl.multiple_of(j * rows, rows), rows)
            wgu_ref[sl, :] = wgu_f32_ref[sl, :].astype(BF16)
            return carry

        def cast_dn(j, carry):
            sl = pl.ds(pl.multiple_of(j * rows, rows), rows)
            wdn_ref[sl, :] = wdn_f32_ref[sl, :].astype(BF16)
            return carry

        lax.fori_loop(0, wgu_ref.shape[0] // rows, cast_gu, 0)
        lax.fori_loop(0, wdn_ref.shape[0] // rows, cast_dn, 0)

    @pl.when(used)
    def _():
        de = wdn_ref.shape[0]
        gu = jnp.dot(xs_ref[...].astype(BF16), wgu_ref[...], preferred_element_type=F32) + bgu_ref[...]
        gate = jnp.minimum(gu[:, :de], SWIGLU_LIMIT)
        up = jnp.clip(gu[:, de:], -SWIGLU_LIMIT, SWIGLU_LIMIT)
        hid = (up + 1.0) * (gate * jax.nn.sigmoid(SWIGLU_ALPHA * gate))
        ys_ref[...] = jnp.dot(hid.astype(BF16), wdn_ref[...], preferred_element_type=F32) + bdn_ref[...]

    @pl.when(jnp.logical_not(used))
    def _():
        ys_ref[...] = jnp.zeros_like(ys_ref)


def _moe_ffn(xs, blk_e, n_used, w_gate_up, b_gu4, w_down, b_dn4, layer, block_rows):
    n_rows, d = xs.shape
    de = w_down.shape[2]
    n_blocks = n_rows // block_rows
    w_map = lambda i, be, nu: (layer, be[i], 0, 0)
    return pl.pallas_call(
        _moe_ffn_kernel,
        grid_spec=pltpu.PrefetchScalarGridSpec(
            num_scalar_prefetch=2,
            grid=(n_blocks,),
            in_specs=[
                pl.BlockSpec((block_rows, d), lambda i, be, nu: (jnp.minimum(i, nu[0] - 1), 0)),
                pl.BlockSpec((None, None, d, 2 * de), w_map),
                pl.BlockSpec((None, None, 1, 2 * de), w_map),
                pl.BlockSpec((None, None, de, d), w_map),
                pl.BlockSpec((None, None, 1, d), w_map),
            ],
            out_specs=pl.BlockSpec((block_rows, d), lambda i, be, nu: (i, 0)),
            scratch_shapes=[pltpu.VMEM((d, 2 * de), BF16), pltpu.VMEM((de, d), BF16)],
        ),
        out_shape=jax.ShapeDtypeStruct((n_rows, d), F32),
        compiler_params=_params("arbitrary"),
        name="moe_ffn",
    )(blk_e, n_used, xs, w_gate_up, b_gu4, w_down, b_dn4)


def _combine_kernel(dest_ref, gate_ref, x1_ref, ys_ref, g2_ref, b2_ref, o_ref, buf0, buf1, sem, *, alpha):
    i = pl.program_id(0)
    n_tiles = pl.num_programs(0) - 1
    tm = x1_ref.shape[0]

    def issue(buf, slot):
        def body(g, carry):
            r0 = pl.multiple_of(g * DMA_UNROLL_ROWS, DMA_UNROLL_ROWS)
            for j in range(DMA_UNROLL_ROWS):
                for kk in range(TOP_K):
                    src = dest_ref[(r0 + j) * TOP_K + kk]
                    pltpu.make_async_copy(ys_ref.at[pl.ds(src, 1)], buf.at[kk, pl.ds(r0 + j, 1)],
                                          sem.at[slot]).start(priority=kk % 2)
            return carry

        lax.fori_loop(0, tm // DMA_UNROLL_ROWS, body, 0)

    def finish(buf, slot):
        for kk in range(TOP_K):
            pltpu.make_async_copy(ys_ref.at[pl.ds(0, tm)], buf.at[kk], sem.at[slot]).wait()
        gates = gate_ref[...]
        m = gates[:, 0:1] * buf[0]
        for kk in range(1, TOP_K):
            m = m + gates[:, kk:kk + 1] * buf[kk]
        o_ref[...] = _layernorm(alpha * x1_ref[...] + m, g2_ref[...], b2_ref[...])

    for parity, (cur, prev) in enumerate(((buf0, buf1), (buf1, buf0))):
        @pl.when(i % 2 == parity)
        def _():
            @pl.when(i < n_tiles)
            def _():
                issue(cur, parity)

            @pl.when(i > 0)
            def _():
                finish(prev, 1 - parity)


def _combine(x1, ys, dest, rgate, ln_g3, ln_b3, layer, alpha):
    t, d = x1.shape
    tm = min(256, t)
    nt = t // tm
    assert tm % DMA_UNROLL_ROWS == 0
    done = lambda i: (jnp.maximum(i - 1, 0), 0)
    return pl.pallas_call(
        functools.partial(_combine_kernel, alpha=alpha),
        grid=(nt + 1,),
        in_specs=[
            pl.BlockSpec((tm * TOP_K,), lambda i: (jnp.minimum(i, nt - 1),), memory_space=pltpu.SMEM),
            pl.BlockSpec((tm, LANES), done),
            pl.BlockSpec((tm, d), done),
            pl.BlockSpec(memory_space=pl.ANY),
            pl.BlockSpec((None, 1, d), lambda i: (layer, 0, 0)),
            pl.BlockSpec((None, 1, d), lambda i: (layer, 0, 0)),
        ],
        out_specs=pl.BlockSpec((tm, d), done),
        out_shape=jax.ShapeDtypeStruct((t, d), F32),
        scratch_shapes=[pltpu.VMEM((TOP_K, tm, d), F32), pltpu.VMEM((TOP_K, tm, d), F32),
                        pltpu.SemaphoreType.DMA((2,))],
        compiler_params=_params("arbitrary"),
        name="moe_combine",
    )(dest, rgate, x1, ys, ln_g3, ln_b3)


def _block_diag_gate_weights(wa, wx):
    n_l, n_dir, n_h, hd, _ = wa.shape
    hh = n_h // 2
    eye = jnp.eye(hh, dtype=wa.dtype)

    def bd(wm):
        wm = wm.reshape(n_l, n_dir, 2, hh, hd, hd)
        full = jnp.einsum("ldghij,hk->ldghikj", wm, eye)
        return full.reshape(n_l, n_dir, 2, hh * hd, hh * hd)

    return jnp.concatenate([bd(wa), bd(wx)], axis=-1).astype(BF16)


def kernel(x, w_in, hg_lb, hg_norm, lru_conv_w, lru_conv_b, lru_wa, lru_ba, lru_wx, lru_bx, lru_lambda, lru_norm,
           w_out, ln1_g, ln1_b, router_w, router_b, w_gate_up, b_gate_up, w_down, b_down, ln2_g, ln2_b):
    bsz, s, d = x.shape
    depth = w_in.shape[0]
    hg_w = hg_lb.shape[-1]
    lru_w = lru_lambda.shape[-1]
    n_exp = router_w.shape[-1]
    de = w_down.shape[2]
    t = bsz * s
    alpha = float((2 * depth) ** 0.25)
    block_rows = MXU_DIM
    assert w_in.shape[-1] == 5 * hg_w + 2 * lru_w and hg_w == lru_w
    assert s % 256 == 0 or s < 256

    p = jax.nn.softmax(hg_lb.astype(F32), axis=0)
    lower_bounds = jnp.clip(jnp.cumsum(p, axis=0) - p[0:1], 0.0, 1.0 - 1e-6)
    w_in_b = w_in.astype(BF16)
    w_out_b = w_out.astype(BF16)
    wg = _block_diag_gate_weights(lru_wa, lru_wx)
    gate_bias = jnp.stack([lru_ba, lru_bx], axis=2).astype(F32)
    row3 = lambda a: a.astype(F32).reshape(depth, 1, a.shape[-1])
    b_gu4 = b_gate_up.astype(F32).reshape(depth, n_exp, 1, 2 * de)
    b_dn4 = b_down.astype(F32).reshape(depth, n_exp, 1, d)

    x2 = x.reshape(t, d)
    for layer in range(depth):
        proj = _in_proj(x2, w_in_b, layer)
        proj3 = proj.reshape(bsz, s, proj.shape[-1])
        o_f, o_b = _hgrn2(proj3, lower_bounds, layer, hg_w)
        h_f, h_b = _griffin(proj3, lru_conv_w, row3(lru_conv_b), wg, gate_bias, lru_lambda, layer, lru_w,
                            col_block=5)
        x1, ridx, rgate, counts_f32 = _mix(x2, o_f.reshape(t, hg_w), o_b.reshape(t, hg_w), h_f.reshape(t, lru_w),
                               h_b.reshape(t, lru_w), proj, w_out_b, row3(hg_norm), row3(lru_norm), row3(ln1_g),
                               row3(ln1_b), router_w, row3(router_b), layer, alpha, zg_block=4, zy_block=6)
        dest, counts, pstarts, blk_e, n_used, n_rows = _routing_tables(ridx, counts_f32, n_exp, block_rows)
        xs = _dispatch(x1, dest, counts, pstarts, n_rows, block_rows)
        ys = _moe_ffn(xs, blk_e, n_used, w_gate_up, b_gu4, w_down, b_dn4, layer, block_rows)
        x2 = _combine(x1, ys, dest, rgate, row3(ln2_g), row3(ln2_b), layer, alpha)
    return x2.reshape(bsz, s, d)
```

```python
import functools

import jax
import jax.numpy as jnp
from jax import lax
from jax.experimental import pallas as pl
from jax.experimental.pallas import tpu as pltpu

F32 = jnp.float32
BF16 = jnp.bfloat16

HG_HEADS = 4
HG_CHUNK = 32
LB_FLOOR = 1e-30
LRU_HEADS = 8
LRU_C = 8.0
CONV_WIDTH = 4
TOP_K = 4
SWIGLU_LIMIT = 7.0
SWIGLU_ALPHA = 1.702
LN_EPS = 1e-5
RMS_EPS = 1e-6

LANES = 128
SUBLANES = 8
MXU_DIM = 256
VMEM_LIMIT_BYTES = 56 * 1024 * 1024

NT_DIMS = (((1,), (1,)), ((), ()))
TN_DIMS = (((0,), (0,)), ((), ()))


def _params(*semantics):
    return pltpu.CompilerParams(dimension_semantics=semantics, vmem_limit_bytes=VMEM_LIMIT_BYTES)


def _bdot(a, b):
    return jnp.dot(a.astype(BF16), b.astype(BF16), preferred_element_type=F32)


def _bdot_general(a, b, dims):
    return lax.dot_general(a.astype(BF16), b.astype(BF16), dims, preferred_element_type=F32)


def _layernorm(t, g, b):
    mu = jnp.mean(t, axis=-1, keepdims=True)
    c = t - mu
    var = jnp.mean(c * c, axis=-1, keepdims=True)
    return c * lax.rsqrt(var + LN_EPS) * g + b


def _in_proj_kernel(x_ref, w_ref, o_ref):
    o_ref[...] = jnp.dot(x_ref[...].astype(BF16), w_ref[...], preferred_element_type=F32)


def _in_proj(x2, w_in_bf16, layer):
    t, d = x2.shape
    n = w_in_bf16.shape[-1]
    tm = min(512, t)
    return pl.pallas_call(
        _in_proj_kernel,
        grid=(t // tm,),
        in_specs=[
            pl.BlockSpec((tm, d), lambda i: (i, 0)),
            pl.BlockSpec((None, d, n), lambda i: (layer, 0, 0)),
        ],
        out_specs=pl.BlockSpec((tm, n), lambda i: (i, 0)),
        out_shape=jax.ShapeDtypeStruct((t, n), F32),
        compiler_params=_params("parallel"),
        name="in_proj",
    )(x2, w_in_bf16)


def _cumsum_rows(x):
    n = x.shape[0]
    row = lax.broadcasted_iota(jnp.int32, x.shape, 0)
    s = 1
    while s < n:
        x = x + jnp.where(row >= s, pltpu.roll(x, s, 0), 0.0)
        s *= 2
    return x


def _hg_chunk(zq, zv, zf, lb, log_lb, log1m_lb, st_ref, o_ref, r0, reverse):
    c = HG_CHUNK
    q = zq * jax.nn.sigmoid(zq)
    e = jnp.exp(-jnp.abs(zf))
    log_sig = jnp.minimum(zf, 0.0) - jnp.log1p(e)
    sig_neg = jnp.where(zf >= 0.0, e, 1.0) / (1.0 + e)
    b = log1m_lb + log_sig
    logf = jnp.maximum(log_lb, b) + jnp.log1p(jnp.exp(-jnp.abs(log_lb - b)))
    k = (1.0 - lb) * sig_neg

    p = _cumsum_rows(logf)
    g_last = p[c - 1:c]
    if reverse:
        g = g_last - p + logf
        g_ref = g[c // 2:c // 2 + 1]
    else:
        g = p
        g_ref = g[c // 2 - 1:c // 2]
    qg = q * jnp.exp(g - g_ref)
    kg = k * jnp.exp(g_ref - g)
    kl = kg * jnp.exp(g_last - g_ref)
    qe = qg * jnp.exp(g_ref)
    dec = jnp.exp(g_last)

    row = lax.broadcasted_iota(jnp.int32, (c, c), 0)
    col = lax.broadcasted_iota(jnp.int32, (c, c), 1)
    keep = (col >= row) if reverse else (col <= row)
    hd = zq.shape[1] // HG_HEADS
    for h in range(HG_HEADS):
        sl = slice(h * hd, (h + 1) * hd)
        scores = jnp.where(keep, _bdot_general(qg[:, sl], kg[:, sl], NT_DIMS), 0.0)
        st = st_ref[h]
        o = _bdot(scores, zv[:, sl]) + _bdot_general(qe[:, sl], st, NT_DIMS)
        o_ref[pl.ds(r0, c), sl] = o
        st_ref[h] = st * dec[:, sl] + _bdot_general(zv[:, sl], kl[:, sl], TN_DIMS)


def _hgrn2_kernel(fqif_ref, bqi_ref, bzb_ref, lb_ref, of_ref, ob_ref, stf_ref, stb_ref):
    @pl.when(pl.program_id(1) == 0)
    def _():
        stf_ref[...] = jnp.zeros_like(stf_ref)
        stb_ref[...] = jnp.zeros_like(stb_ref)

    w = lb_ref.shape[1]
    lb = lb_ref[...]
    log_lb = jnp.log(jnp.maximum(lb, LB_FLOOR))
    log1m_lb = jnp.log1p(-lb)
    n_chunks = of_ref.shape[0] // HG_CHUNK

    def body(ci, carry):
        r0 = pl.multiple_of(ci * HG_CHUNK, HG_CHUNK)
        rows = pl.ds(r0, HG_CHUNK)
        _hg_chunk(fqif_ref[rows, 0:w], fqif_ref[rows, w:2 * w], fqif_ref[rows, 2 * w:3 * w],
                  lb[0:1], log_lb[0:1], log1m_lb[0:1], stf_ref, of_ref, r0, False)
        rb = pl.multiple_of((n_chunks - 1 - ci) * HG_CHUNK, HG_CHUNK)
        rows_b = pl.ds(rb, HG_CHUNK)
        _hg_chunk(bqi_ref[rows_b, 0:w], bqi_ref[rows_b, w:2 * w], bzb_ref[rows_b, :],
                  lb[1:2], log_lb[1:2], log1m_lb[1:2], stb_ref, ob_ref, rb, True)
        return carry

    lax.fori_loop(0, n_chunks, body, 0, unroll=4)


def _hgrn2(proj3, lower_bounds, layer, hg_width):
    bsz, s, _ = proj3.shape
    w = hg_width
    ts = min(256, s)
    ns = s // ts
    hd = w // HG_HEADS
    out = jax.ShapeDtypeStruct((bsz, s, w), F32)
    state = pltpu.VMEM((HG_HEADS, hd, hd), F32)
    return pl.pallas_call(
        _hgrn2_kernel,
        grid=(bsz, ns),
        in_specs=[
            pl.BlockSpec((None, ts, 3 * w), lambda b, j: (b, j, 0)),
            pl.BlockSpec((None, ts, 2 * w), lambda b, j: (b, ns - 1 - j, 0)),
            pl.BlockSpec((None, ts, w), lambda b, j: (b, ns - 1 - j, 3)),
            pl.BlockSpec((None, 2, w), lambda b, j: (layer, 0, 0)),
        ],
        out_specs=[
            pl.BlockSpec((None, ts, w), lambda b, j: (b, j, 0)),
            pl.BlockSpec((None, ts, w), lambda b, j: (b, ns - 1 - j, 0)),
        ],
        out_shape=[out, out],
        scratch_shapes=[state, state],
        compiler_params=_params("parallel", "arbitrary"),
        name="hgrn2",
    )(proj3, proj3, proj3, lower_bounds)


def _lin_scan(a, u, carry, h_ref, reverse):
    n, w = a.shape
    n_groups = n // SUBLANES
    a = a.reshape(n_groups, SUBLANES, w)
    u = u.reshape(n_groups, SUBLANES, w)
    sub = lax.broadcasted_iota(jnp.int32, a.shape, 1)
    s = 1
    while s < SUBLANES:
        shift = (SUBLANES - s) if reverse else s
        m = (sub < SUBLANES - s) if reverse else (sub >= s)
        a_sh = jnp.where(m, pltpu.roll(a, shift, 1), 1.0)
        u_sh = jnp.where(m, pltpu.roll(u, shift, 1), 0.0)
        u = u + a * u_sh
        a = a * a_sh
        s *= 2
    groups = range(n_groups)
    for g in (reversed(groups) if reverse else groups):
        hg = u[g] + a[g] * carry
        h_ref[g * SUBLANES:(g + 1) * SUBLANES, :] = hg
        carry = hg[0:1] if reverse else hg[SUBLANES - 1:SUBLANES]
    return carry


def _griffin_dir(main_ref, prev_ref, next_ref, is_first, is_last, ext_ref, cw, cb, wg_ref, bias, sp,
                 carry_ref, h_ref, reverse):
    ts, w = main_ref.shape
    halo = SUBLANES
    ext_ref[0:halo, :] = jnp.where(is_first, 0.0, prev_ref[...])
    ext_ref[halo:halo + ts, :] = main_ref[...]
    ext_ref[halo + ts:2 * halo + ts, :] = jnp.where(is_last, 0.0, next_ref[...])
    xc = cb
    for j in range(CONV_WIDTH):
        xc = xc + cw[j:j + 1] * ext_ref[halo - 2 + j:halo - 2 + j + ts, :]

    half = w // 2
    pre = [_bdot(xc[:, i * half:(i + 1) * half], wg_ref[i]) for i in range(2)]
    r_pre = jnp.concatenate([pre[0][:, :half], pre[1][:, :half]], axis=1) + bias[0:1]
    i_pre = jnp.concatenate([pre[0][:, half:], pre[1][:, half:]], axis=1) + bias[1:2]
    r = jax.nn.sigmoid(r_pre)
    ig = jax.nn.sigmoid(i_pre)
    log_a = (-LRU_C) * r * sp
    a = jnp.exp(log_a)
    u = jnp.sqrt(jnp.maximum(1.0 - a * a, 0.0)) * (ig * xc)
    carry_ref[0:1, :] = _lin_scan(a, u, carry_ref[0:1, :], h_ref, reverse)


def _griffin_kernel(fm_ref, fp_ref, fn_ref, bm_ref, bp_ref, bn_ref, cw_ref, cb_ref, wg_ref, bias_ref, lam_ref,
                    hf_ref, hb_ref, ext_ref, cf_ref, cbk_ref):
    j = pl.program_id(1)
    ns = pl.num_programs(1)

    @pl.when(j == 0)
    def _():
        cf_ref[...] = jnp.zeros_like(cf_ref)
        cbk_ref[...] = jnp.zeros_like(cbk_ref)

    cw = cw_ref[...]
    cb = cb_ref[...]
    sp = jax.nn.softplus(-lam_ref[...])
    _griffin_dir(fm_ref, fp_ref, fn_ref, j == 0, j == ns - 1, ext_ref, cw, cb, wg_ref.at[0], bias_ref[0], sp[0:1],
                 cf_ref, hf_ref, False)
    _griffin_dir(bm_ref, bp_ref, bn_ref, j == ns - 1, j == 0, ext_ref, cw, cb, wg_ref.at[1], bias_ref[1], sp[1:2],
                 cbk_ref, hb_ref, True)


def _griffin(proj3, conv_w, conv_b3, wg, gate_bias, lam, layer, lru_width, col_block):
    bsz, s, _ = proj3.shape
    w = lru_width
    ts = min(256, s)
    ns = s // ts
    tb = ts // SUBLANES
    nb8 = s // SUBLANES
    out = jax.ShapeDtypeStruct((bsz, s, w), F32)
    halo = (None, SUBLANES, w)
    return pl.pallas_call(
        _griffin_kernel,
        grid=(bsz, ns),
        in_specs=[
            pl.BlockSpec((None, ts, w), lambda b, j: (b, j, col_block)),
            pl.BlockSpec(halo, lambda b, j: (b, jnp.maximum(j * tb - 1, 0), col_block)),
            pl.BlockSpec(halo, lambda b, j: (b, jnp.minimum((j + 1) * tb, nb8 - 1), col_block)),
            pl.BlockSpec((None, ts, w), lambda b, j: (b, ns - 1 - j, col_block)),
            pl.BlockSpec(halo, lambda b, j: (b, jnp.maximum((ns - 1 - j) * tb - 1, 0), col_block)),
            pl.BlockSpec(halo, lambda b, j: (b, jnp.minimum((ns - j) * tb, nb8 - 1), col_block)),
            pl.BlockSpec((None, CONV_WIDTH, w), lambda b, j: (layer, 0, 0)),
            pl.BlockSpec((None, 1, w), lambda b, j: (layer, 0, 0)),
            pl.BlockSpec((None, 2, 2, w // 2, w), lambda b, j: (layer, 0, 0, 0, 0)),
            pl.BlockSpec((None, 2, 2, w), lambda b, j: (layer, 0, 0, 0)),
            pl.BlockSpec((None, 2, w), lambda b, j: (layer, 0, 0)),
        ],
        out_specs=[
            pl.BlockSpec((None, ts, w), lambda b, j: (b, j, 0)),
            pl.BlockSpec((None, ts, w), lambda b, j: (b, ns - 1 - j, 0)),
        ],
        out_shape=[out, out],
        scratch_shapes=[
            pltpu.VMEM((ts + 2 * SUBLANES, w), F32),
            pltpu.VMEM((SUBLANES, w), F32),
            pltpu.VMEM((SUBLANES, w), F32),
        ],
        compiler_params=_params("parallel", "arbitrary"),
        name="griffin",
    )(proj3, proj3, proj3, proj3, proj3, proj3, conv_w, conv_b3, wg, gate_bias, lam)


def _gelu_tanh(x):
    return 0.5 * x * (1.0 + jnp.tanh(0.7978845608028654 * (x + 0.044715 * x * x * x)))


def _mix_kernel(x_ref, of_ref, ob_ref, zg_ref, hf_ref, hb_ref, zy_ref, wout_ref, hgn_ref, lrn_ref, g1_ref, b1_ref,
                rw_ref, rb_ref, tri_ref, x1_ref, ridx_ref, rgate_ref, counts_ref, cnt_ref, *, alpha):
    @pl.when(pl.program_id(0) == 0)
    def _():
        cnt_ref[...] = jnp.zeros_like(cnt_ref)

    o = of_ref[...] + ob_ref[...]
    w = o.shape[1]
    hd = w // HG_HEADS
    parts = []
    for h in range(HG_HEADS):
        oh = o[:, h * hd:(h + 1) * hd]
        parts.append(oh * lax.rsqrt(jnp.mean(oh * oh, axis=-1, keepdims=True) + RMS_EPS))
    zg = zg_ref[...]
    o_hg = jnp.concatenate(parts, axis=1) * hgn_ref[...] * (zg * jax.nn.sigmoid(zg))
    hh = hf_ref[...] + hb_ref[...]
    o_lru = hh * lax.rsqrt(jnp.mean(hh * hh, axis=-1, keepdims=True) + RMS_EPS) * lrn_ref[...]
    o_lru = o_lru * _gelu_tanh(zy_ref[...])
    y = _bdot(o_hg, wout_ref[0:w, :]) + _bdot(o_lru, wout_ref[w:, :])
    x1 = _layernorm(alpha * x_ref[...] + y, g1_ref[...], b1_ref[...])
    x1_ref[...] = x1

    rw = rw_ref[...]
    x_hi, w_hi = x1.astype(BF16), rw.astype(BF16)
    x_lo = (x1 - x_hi.astype(F32)).astype(BF16)
    w_lo = (rw - w_hi.astype(F32)).astype(BF16)
    logits = (jnp.dot(x_hi, w_hi, preferred_element_type=F32) + jnp.dot(x_lo, w_hi, preferred_element_type=F32)
              + jnp.dot(x_hi, w_lo, preferred_element_type=F32) + rb_ref[...])
    n_exp = logits.shape[1]
    eid = lax.broadcasted_iota(jnp.int32, logits.shape, 1)
    vals, idxs = [], []
    for _ in range(TOP_K):
        m = jnp.max(logits, axis=-1, keepdims=True)
        idx = jnp.min(jnp.where(logits == m, eid, n_exp), axis=-1, keepdims=True)
        vals.append(m)
        idxs.append(idx)
        logits = jnp.where(eid == idx, -jnp.inf, logits)
    exps = [jnp.exp(v - vals[0]) for v in vals]
    denom = exps[0] + exps[1] + exps[2] + exps[3]

    base = cnt_ref[0:1, 0:n_exp]
    ranks = []
    for kk in range(TOP_K):
        onehot = (eid == idxs[kk]).astype(F32)
        before = jnp.dot(tri_ref[...], onehot.astype(BF16), preferred_element_type=F32) + base
        ranks.append(jnp.sum(onehot * before, axis=-1, keepdims=True).astype(jnp.int32))
        base = base + jnp.sum(onehot, axis=0, keepdims=True)
    cnt_ref[0:1, 0:n_exp] = base
    counts_ref[...] = cnt_ref[...]

    lane = lax.broadcasted_iota(jnp.int32, ridx_ref.shape, 1)
    ridx = jnp.zeros(ridx_ref.shape, jnp.int32)
    rgate = jnp.zeros(rgate_ref.shape, F32)
    for kk in range(TOP_K):
        ridx = jnp.where(lane == kk, idxs[kk], ridx)
        ridx = jnp.where(lane == TOP_K + kk, ranks[kk], ridx)
        rgate = jnp.where(lane == kk, exps[kk] / denom, rgate)
    ridx_ref[...] = ridx
    rgate_ref[...] = rgate


def _mix(x2, o_f, o_b, h_f, h_b, proj, w_out_bf16, hg_norm3, lru_norm3, ln_g3, ln_b3, router_w, router_b3, layer,
         alpha, zg_block, zy_block):
    t, d = x2.shape
    w = o_f.shape[1]
    n_exp = router_w.shape[-1]
    tm = min(512, t)
    row_blk = lambda width: pl.BlockSpec((tm, width), lambda i: (i, 0))
    vec = lambda width: pl.BlockSpec((None, 1, width), lambda i: (layer, 0, 0))
    return pl.pallas_call(
        functools.partial(_mix_kernel, alpha=alpha),
        grid=(t // tm,),
        in_specs=[
            row_blk(d), row_blk(w), row_blk(w),
            pl.BlockSpec((tm, w), lambda i: (i, zg_block)),
            row_blk(w), row_blk(w),
            pl.BlockSpec((tm, w), lambda i: (i, zy_block)),
            pl.BlockSpec((None, 2 * w, d), lambda i: (layer, 0, 0)),
            vec(w), vec(w), vec(d), vec(d),
            pl.BlockSpec((None, d, n_exp), lambda i: (layer, 0, 0)),
            vec(n_exp),
            pl.BlockSpec((tm, tm), lambda i: (0, 0)),
        ],
        out_specs=[row_blk(d), row_blk(LANES), row_blk(LANES), pl.BlockSpec((SUBLANES, LANES), lambda i: (0, 0))],
        out_shape=[
            jax.ShapeDtypeStruct((t, d), F32),
            jax.ShapeDtypeStruct((t, LANES), jnp.int32),
            jax.ShapeDtypeStruct((t, LANES), F32),
            jax.ShapeDtypeStruct((SUBLANES, LANES), F32),
        ],
        scratch_shapes=[pltpu.VMEM((SUBLANES, LANES), F32)],
        compiler_params=_params("arbitrary"),
        name="mix_ln_router",
    )(x2, o_f, o_b, proj, h_f, h_b, proj, w_out_bf16, hg_norm3, lru_norm3, ln_g3, ln_b3, router_w, router_b3,
      jnp.tril(jnp.ones((tm, tm), BF16), k=-1))


DMA_UNROLL_ROWS = 32


def _routing_tables(ridx, counts_f32, n_exp, block_rows):
    t = ridx.shape[0]
    idx = ridx[:, :TOP_K]
    rank = ridx[:, TOP_K:2 * TOP_K]
    counts = counts_f32[0, :n_exp].astype(jnp.int32)
    padded = ((counts + block_rows - 1) // block_rows) * block_rows
    pends = jnp.cumsum(padded)
    pstarts = pends - padded
    onehot = idx[:, :, None] == jnp.arange(n_exp, dtype=jnp.int32)[None, None, :]
    dest = (jnp.sum(jnp.where(onehot, pstarts[None, None, :], 0), axis=-1) + rank).reshape(-1).astype(jnp.int32)
    n_rows = t * TOP_K + n_exp * block_rows
    n_blocks = n_rows // block_rows
    blk_start = jnp.arange(n_blocks, dtype=jnp.int32) * block_rows
    blk_e = jnp.minimum(jnp.searchsorted(pends, blk_start, side="right", method="compare_all"),
                        n_exp - 1).astype(jnp.int32)
    n_used = (pends[-1] // block_rows).astype(jnp.int32).reshape(1)
    return dest, counts, pstarts.astype(jnp.int32), blk_e, n_used, n_rows


def _dispatch_kernel(cnt_ref, pst_ref, dest_ref, x_ref, xs_ref, zero_ref, sem, *, block_rows):
    tm = x_ref.shape[0]

    def issue(g, carry):
        r0 = pl.multiple_of(g * DMA_UNROLL_ROWS, DMA_UNROLL_ROWS)
        for j in range(DMA_UNROLL_ROWS):
            for kk in range(TOP_K):
                d = dest_ref[(r0 + j) * TOP_K + kk]
                pltpu.make_async_copy(x_ref.at[pl.ds(r0 + j, 1)], xs_ref.at[pl.ds(d, 1)], sem).start(priority=kk % 2)
        return carry

    lax.fori_loop(0, tm // DMA_UNROLL_ROWS, issue, 0)
    for _ in range(TOP_K):
        pltpu.make_async_copy(x_ref, xs_ref.at[pl.ds(0, tm)], sem).wait()

    @pl.when(pl.program_id(0) == pl.num_programs(0) - 1)
    def _():
        zero_ref[...] = jnp.zeros_like(zero_ref)
        n_exp = cnt_ref.shape[0]
        last = n_exp - 1
        cnt_last = cnt_ref[last]
        used_rows = pst_ref[last] + cnt_last + (block_rows - cnt_last % block_rows) % block_rows
        n_tail = (xs_ref.shape[0] - used_rows) // block_rows

        def tail_copy(b):
            start = pl.multiple_of(used_rows + b * block_rows, block_rows)
            return pltpu.make_async_copy(zero_ref, xs_ref.at[pl.ds(start, block_rows)], sem)

        def tail_start(b, c2):
            tail_copy(b).start()
            return c2

        def tail_wait(b, c2):
            tail_copy(b).wait()
            return c2

        lax.fori_loop(0, n_tail, tail_start, 0)
        lax.fori_loop(0, n_tail, tail_wait, 0)

        def per_expert(e, carry):
            cnt = cnt_ref[e]
            first = pst_ref[e] + cnt
            n_pad = (block_rows - cnt % block_rows) % block_rows

            def pieces(action):
                off = first
                for bit in range(block_rows.bit_length() - 1):
                    size = 1 << bit
                    if size < SUBLANES:
                        copies = [pltpu.make_async_copy(zero_ref.at[pl.ds(0, 1)], xs_ref.at[pl.ds(off + r, 1)], sem)
                                  for r in range(size)]
                    else:
                        start = pl.multiple_of(off, SUBLANES)
                        copies = [pltpu.make_async_copy(zero_ref.at[pl.ds(0, size)], xs_ref.at[pl.ds(start, size)],
                                                        sem)]

                    @pl.when((n_pad & size) != 0)
                    def _():
                        for copy in copies:
                            action(copy)

                    off = off + (n_pad & size)

            pieces(lambda copy: copy.start())
            pieces(lambda copy: copy.wait())
            return carry

        lax.fori_loop(0, n_exp, per_expert, 0)


def _dispatch(x1, dest, counts, pstarts, n_rows, block_rows):
    t, d = x1.shape
    tm = min(512, t)
    assert tm % DMA_UNROLL_ROWS == 0 and block_rows & (block_rows - 1) == 0
    return pl.pallas_call(
        functools.partial(_dispatch_kernel, block_rows=block_rows),
        grid_spec=pltpu.PrefetchScalarGridSpec(
            num_scalar_prefetch=2,
            grid=(t // tm,),
            in_specs=[
                pl.BlockSpec((tm * TOP_K,), lambda i, c, p: (i,), memory_space=pltpu.SMEM),
                pl.BlockSpec((tm, d), lambda i, c, p: (i, 0)),
            ],
            out_specs=pl.BlockSpec(memory_space=pl.ANY),
            scratch_shapes=[pltpu.VMEM((block_rows, d), F32), pltpu.SemaphoreType.DMA],
        ),
        out_shape=jax.ShapeDtypeStruct((n_rows, d), F32),
        compiler_params=_params("arbitrary"),
        name="moe_dispatch",
    )(counts, pstarts, dest, x1)


def _moe_ffn_kernel(blk_e_ref, n_used_ref, xs_ref, wgu_f32_ref, bgu_ref, wdn_f32_ref, bdn_ref, ys_ref,
                    wgu_ref, wdn_ref):
    i = pl.program_id(0)
    used = i < n_used_ref[0]

    @pl.when(used & ((i == 0) | (blk_e_ref[i] != blk_e_ref[jnp.maximum(i - 1, 0)])))
    def _():
        rows = LANES

        def cast_gu(j, carry):
            sl = pl.ds(pl.multiple_of(j * rows, rows), rows)
            wgu_ref[sl, :] = wgu_f32_ref[sl, :].astype(BF16)
            return carry

        def cast_dn(j, carry):
            sl = pl.ds(pl.multiple_of(j * rows, rows), rows)
            wdn_ref[sl, :] = wdn_f32_ref[sl, :].astype(BF16)
            return carry

        lax.fori_loop(0, wgu_ref.shape[0] // rows, cast_gu, 0)
        lax.fori_loop(0, wdn_ref.shape[0] // rows, cast_dn, 0)

    @pl.when(used)
    def _():
        de = wdn_ref.shape[0]
        gu = jnp.dot(xs_ref[...].astype(BF16), wgu_ref[...], preferred_element_type=F32) + bgu_ref[...]
        gate = jnp.minimum(gu[:, :de], SWIGLU_LIMIT)
        up = jnp.clip(gu[:, de:], -SWIGLU_LIMIT, SWIGLU_LIMIT)
        hid = (up + 1.0) * (gate * jax.nn.sigmoid(SWIGLU_ALPHA * gate))
        ys_ref[...] = jnp.dot(hid.astype(BF16), wdn_ref[...], preferred_element_type=F32) + bdn_ref[...]

    @pl.when(jnp.logical_not(used))
    def _():
        ys_ref[...] = jnp.zeros_like(ys_ref)


def _moe_ffn(xs, blk_e, n_used, w_gate_up, b_gu4, w_down, b_dn4, layer, block_rows):
    n_rows, d = xs.shape
    de = w_down.shape[2]
    n_blocks = n_rows // block_rows
    w_map = lambda i, be, nu: (layer, be[i], 0, 0)
    return pl.pallas_call(
        _moe_ffn_kernel,
        grid_spec=pltpu.PrefetchScalarGridSpec(
            num_scalar_prefetch=2,
            grid=(n_blocks,),
            in_specs=[
                pl.BlockSpec((block_rows, d), lambda i, be, nu: (jnp.minimum(i, nu[0] - 1), 0)),
                pl.BlockSpec((None, None, d, 2 * de), w_map),
                pl.BlockSpec((None, None, 1, 2 * de), w_map),
                pl.BlockSpec((None, None, de, d), w_map),
                pl.BlockSpec((None, None, 1, d), w_map),
            ],
            out_specs=pl.BlockSpec((block_rows, d), lambda i, be, nu: (i, 0)),
            scratch_shapes=[pltpu.VMEM((d, 2 * de), BF16), pltpu.VMEM((de, d), BF16)],
        ),
        out_shape=jax.ShapeDtypeStruct((n_rows, d), F32),
        compiler_params=_params("arbitrary"),
        name="moe_ffn",
    )(blk_e, n_used, xs, w_gate_up, b_gu4, w_down, b_dn4)


def _combine_kernel(dest_ref, gate_ref, x1_ref, ys_ref, g2_ref, b2_ref, o_ref, buf0, buf1, sem, *, alpha):
    i = pl.program_id(0)
    n_tiles = pl.num_programs(0) - 1
    tm = x1_ref.shape[0]

    def issue(buf, slot):
        def body(g, carry):
            r0 = pl.multiple_of(g * DMA_UNROLL_ROWS, DMA_UNROLL_ROWS)
            for j in range(DMA_UNROLL_ROWS):
                for kk in range(TOP_K):
                    src = dest_ref[(r0 + j) * TOP_K + kk]
                    pltpu.make_async_copy(ys_ref.at[pl.ds(src, 1)], buf.at[kk, pl.ds(r0 + j, 1)],
                                          sem.at[slot]).start(priority=kk % 2)
            return carry

        lax.fori_loop(0, tm // DMA_UNROLL_ROWS, body, 0)

    def finish(buf, slot):
        for kk in range(TOP_K):
            pltpu.make_async_copy(ys_ref.at[pl.ds(0, tm)], buf.at[kk], sem.at[slot]).wait()
        gates = gate_ref[...]
        m = gates[:, 0:1] * buf[0]
        for kk in range(1, TOP_K):
            m = m + gates[:, kk:kk + 1] * buf[kk]
        o_ref[...] = _layernorm(alpha * x1_ref[...] + m, g2_ref[...], b2_ref[...])

    for parity, (cur, prev) in enumerate(((buf0, buf1), (buf1, buf0))):
        @pl.when(i % 2 == parity)
        def _():
            @pl.when(i < n_tiles)
            def _():
                issue(cur, parity)

            @pl.when(i > 0)
            def _():
                finish(prev, 1 - parity)


def _combine(x1, ys, dest, rgate, ln_g3, ln_b3, layer, alpha):
    t, d = x1.shape
    tm = min(256, t)
    nt = t // tm
    assert tm % DMA_UNROLL_ROWS == 0
    done = lambda i: (jnp.maximum(i - 1, 0), 0)
    return pl.pallas_call(
        functools.partial(_combine_kernel, alpha=alpha),
        grid=(nt + 1,),
        in_specs=[
            pl.BlockSpec((tm * TOP_K,), lambda i: (jnp.minimum(i, nt - 1),), memory_space=pltpu.SMEM),
            pl.BlockSpec((tm, LANES), done),
            pl.BlockSpec((tm, d), done),
            pl.BlockSpec(memory_space=pl.ANY),
            pl.BlockSpec((None, 1, d), lambda i: (layer, 0, 0)),
            pl.BlockSpec((None, 1, d), lambda i: (layer, 0, 0)),
        ],
        out_specs=pl.BlockSpec((tm, d), done),
        out_shape=jax.ShapeDtypeStruct((t, d), F32),
        scratch_shapes=[pltpu.VMEM((TOP_K, tm, d), F32), pltpu.VMEM((TOP_K, tm, d), F32),
                        pltpu.SemaphoreType.DMA((2,))],
        compiler_params=_params("arbitrary"),
        name="moe_combine",
    )(dest, rgate, x1, ys, ln_g3, ln_b3)


def _block_diag_gate_weights(wa, wx):
    n_l, n_dir, n_h, hd, _ = wa.shape
    hh = n_h // 2
    eye = jnp.eye(hh, dtype=wa.dtype)

    def bd(wm):
        wm = wm.reshape(n_l, n_dir, 2, hh, hd, hd)
        full = jnp.einsum("ldghij,hk->ldghikj", wm, eye)
        return full.reshape(n_l, n_dir, 2, hh * hd, hh * hd)

    return jnp.concatenate([bd(wa), bd(wx)], axis=-1).astype(BF16)


def kernel(x, w_in, hg_lb, hg_norm, lru_conv_w, lru_conv_b, lru_wa, lru_ba, lru_wx, lru_bx, lru_lambda, lru_norm,
           w_out, ln1_g, ln1_b, router_w, router_b, w_gate_up, b_gate_up, w_down, b_down, ln2_g, ln2_b):
    bsz, s, d = x.shape
    depth = w_in.shape[0]
    hg_w = hg_lb.shape[-1]
    lru_w = lru_lambda.shape[-1]
    n_exp = router_w.shape[-1]
    de = w_down.shape[2]
    t = bsz * s
    alpha = float((2 * depth) ** 0.25)
    block_rows = 2 * MXU_DIM
    assert w_in.shape[-1] == 5 * hg_w + 2 * lru_w and hg_w == lru_w
    assert s % 256 == 0 or s < 256

    p = jax.nn.softmax(hg_lb.astype(F32), axis=0)
    lower_bounds = jnp.clip(jnp.cumsum(p, axis=0) - p[0:1], 0.0, 1.0 - 1e-6)
    w_in_b = w_in.astype(BF16)
    w_out_b = w_out.astype(BF16)
    wg = _block_diag_gate_weights(lru_wa, lru_wx)
    gate_bias = jnp.stack([lru_ba, lru_bx], axis=2).astype(F32)
    row3 = lambda a: a.astype(F32).reshape(depth, 1, a.shape[-1])
    b_gu4 = b_gate_up.astype(F32).reshape(depth, n_exp, 1, 2 * de)
    b_dn4 = b_down.astype(F32).reshape(depth, n_exp, 1, d)

    x2 = x.reshape(t, d)
    for layer in range(depth):
        proj = _in_proj(x2, w_in_b, layer)
        proj3 = proj.reshape(bsz, s, proj.shape[-1])
        o_f, o_b = _hgrn2(proj3, lower_bounds, layer, hg_w)
        h_f, h_b = _griffin(proj3, lru_conv_w, row3(lru_conv_b), wg, gate_bias, lru_lambda, layer, lru_w,
                            col_block=5)
        x1, ridx, rgate, counts_f32 = _mix(x2, o_f.reshape(t, hg_w), o_b.reshape(t, hg_w), h_f.reshape(t, lru_w),
                               h_b.reshape(t, lru_w), proj, w_out_b, row3(hg_norm), row3(lru_norm), row3(ln1_g),
                               row3(ln1_b), router_w, row3(router_b), layer, alpha, zg_block=4, zy_block=6)
        dest, counts, pstarts, blk_e, n_used, n_rows = _routing_tables(ridx, counts_f32, n_exp, block_rows)
        xs = _dispatch(x1, dest, counts, pstarts, n_rows, block_rows)
        ys = _moe_ffn(xs, blk_e, n_used, w_gate_up, b_gu4, w_down, b_dn4, layer, block_rows)
        x2 = _combine(x1, ys, dest, rgate, row3(ln2_g), row3(ln2_b), layer, alpha)
    return x2.reshape(bsz, s, d)
```

```python
import functools

import jax
import jax.numpy as jnp
from jax import lax
from jax.experimental import pallas as pl
from jax.experimental.pallas import tpu as pltpu

F32 = jnp.float32
BF16 = jnp.bfloat16

HG_HEADS = 4
HG_CHUNK = 32
LB_FLOOR = 1e-30
LRU_HEADS = 8
LRU_C = 8.0
CONV_WIDTH = 4
TOP_K = 4
SWIGLU_LIMIT = 7.0
SWIGLU_ALPHA = 1.702
LN_EPS = 1e-5
RMS_EPS = 1e-6

LANES = 128
SUBLANES = 8
MXU_DIM = 256
VMEM_LIMIT_BYTES = 56 * 1024 * 1024

NT_DIMS = (((1,), (1,)), ((), ()))
TN_DIMS = (((0,), (0,)), ((), ()))


def _params(*semantics):
    return pltpu.CompilerParams(dimension_semantics=semantics, vmem_limit_bytes=VMEM_LIMIT_BYTES)


def _bdot(a, b):
    return jnp.dot(a.astype(BF16), b.astype(BF16), preferred_element_type=F32)


def _bdot_general(a, b, dims):
    return lax.dot_general(a.astype(BF16), b.astype(BF16), dims, preferred_element_type=F32)


def _layernorm(t, g, b):
    mu = jnp.mean(t, axis=-1, keepdims=True)
    c = t - mu
    var = jnp.mean(c * c, axis=-1, keepdims=True)
    return c * lax.rsqrt(var + LN_EPS) * g + b


def _load_row_tiles(ref, n_rows):
    per_row = ref.shape[0] // n_rows
    return jnp.concatenate([ref[pl.ds(j, n_rows, stride=per_row), :] for j in range(per_row)], axis=1)


def _store_row_tiles(ref, value):
    n_rows, d = value.shape
    per_row = d // LANES
    for j in range(per_row):
        ref[pl.ds(j, n_rows, stride=per_row), :] = value[:, j * LANES:(j + 1) * LANES]


def _in_proj_kernel(x_ref, w_ref, o_ref):
    o_ref[...] = jnp.dot(x_ref[...].astype(BF16), w_ref[...], preferred_element_type=F32)


def _in_proj(x2, w_in_bf16, layer):
    t, d = x2.shape
    n = w_in_bf16.shape[-1]
    tm = min(512, t)
    return pl.pallas_call(
        _in_proj_kernel,
        grid=(t // tm,),
        in_specs=[
            pl.BlockSpec((tm, d), lambda i: (i, 0)),
            pl.BlockSpec((None, d, n), lambda i: (layer, 0, 0)),
        ],
        out_specs=pl.BlockSpec((tm, n), lambda i: (i, 0)),
        out_shape=jax.ShapeDtypeStruct((t, n), F32),
        compiler_params=_params("parallel"),
        name="in_proj",
    )(x2, w_in_bf16)


def _cumsum_rows(x):
    n = x.shape[0]
    row = lax.broadcasted_iota(jnp.int32, x.shape, 0)
    s = 1
    while s < n:
        x = x + jnp.where(row >= s, pltpu.roll(x, s, 0), 0.0)
        s *= 2
    return x


def _hg_chunk(zq, zv, zf, lb, log_lb, log1m_lb, st_ref, o_ref, r0, reverse):
    c = HG_CHUNK
    q = zq * jax.nn.sigmoid(zq)
    e = jnp.exp(-jnp.abs(zf))
    log_sig = jnp.minimum(zf, 0.0) - jnp.log1p(e)
    sig_neg = jnp.where(zf >= 0.0, e, 1.0) / (1.0 + e)
    b = log1m_lb + log_sig
    logf = jnp.maximum(log_lb, b) + jnp.log1p(jnp.exp(-jnp.abs(log_lb - b)))
    k = (1.0 - lb) * sig_neg

    p = _cumsum_rows(logf)
    g_last = p[c - 1:c]
    if reverse:
        g = g_last - p + logf
        g_ref = g[c // 2:c // 2 + 1]
    else:
        g = p
        g_ref = g[c // 2 - 1:c // 2]
    qg = q * jnp.exp(g - g_ref)
    kg = k * jnp.exp(g_ref - g)
    kl = kg * jnp.exp(g_last - g_ref)
    qe = qg * jnp.exp(g_ref)
    dec = jnp.exp(g_last)

    row = lax.broadcasted_iota(jnp.int32, (c, c), 0)
    col = lax.broadcasted_iota(jnp.int32, (c, c), 1)
    keep = (col >= row) if reverse else (col <= row)
    hd = zq.shape[1] // HG_HEADS
    for h in range(HG_HEADS):
        sl = slice(h * hd, (h + 1) * hd)
        scores = jnp.where(keep, _bdot_general(qg[:, sl], kg[:, sl], NT_DIMS), 0.0)
        st = st_ref[h]
        o = _bdot(scores, zv[:, sl]) + _bdot_general(qe[:, sl], st, NT_DIMS)
        o_ref[pl.ds(r0, c), sl] = o
        st_ref[h] = st * dec[:, sl] + _bdot_general(zv[:, sl], kl[:, sl], TN_DIMS)


def _hgrn2_kernel(fqif_ref, bqi_ref, bzb_ref, lb_ref, of_ref, ob_ref, stf_ref, stb_ref):
    @pl.when(pl.program_id(1) == 0)
    def _():
        stf_ref[...] = jnp.zeros_like(stf_ref)
        stb_ref[...] = jnp.zeros_like(stb_ref)

    w = lb_ref.shape[1]
    lb = lb_ref[...]
    log_lb = jnp.log(jnp.maximum(lb, LB_FLOOR))
    log1m_lb = jnp.log1p(-lb)
    n_chunks = of_ref.shape[0] // HG_CHUNK

    def body(ci, carry):
        r0 = pl.multiple_of(ci * HG_CHUNK, HG_CHUNK)
        rows = pl.ds(r0, HG_CHUNK)
        _hg_chunk(fqif_ref[rows, 0:w], fqif_ref[rows, w:2 * w], fqif_ref[rows, 2 * w:3 * w],
                  lb[0:1], log_lb[0:1], log1m_lb[0:1], stf_ref, of_ref, r0, False)
        rb = pl.multiple_of((n_chunks - 1 - ci) * HG_CHUNK, HG_CHUNK)
        rows_b = pl.ds(rb, HG_CHUNK)
        _hg_chunk(bqi_ref[rows_b, 0:w], bqi_ref[rows_b, w:2 * w], bzb_ref[rows_b, :],
                  lb[1:2], log_lb[1:2], log1m_lb[1:2], stb_ref, ob_ref, rb, True)
        return carry

    lax.fori_loop(0, n_chunks, body, 0, unroll=4)


def _hgrn2(proj3, lower_bounds, layer, hg_width):
    bsz, s, _ = proj3.shape
    w = hg_width
    ts = min(256, s)
    ns = s // ts
    hd = w // HG_HEADS
    out = jax.ShapeDtypeStruct((bsz, s, w), F32)
    state = pltpu.VMEM((HG_HEADS, hd, hd), F32)
    return pl.pallas_call(
        _hgrn2_kernel,
        grid=(bsz, ns),
        in_specs=[
            pl.BlockSpec((None, ts, 3 * w), lambda b, j: (b, j, 0)),
            pl.BlockSpec((None, ts, 2 * w), lambda b, j: (b, ns - 1 - j, 0)),
            pl.BlockSpec((None, ts, w), lambda b, j: (b, ns - 1 - j, 3)),
            pl.BlockSpec((None, 2, w), lambda b, j: (layer, 0, 0)),
        ],
        out_specs=[
            pl.BlockSpec((None, ts, w), lambda b, j: (b, j, 0)),
            pl.BlockSpec((None, ts, w), lambda b, j: (b, ns - 1 - j, 0)),
        ],
        out_shape=[out, out],
        scratch_shapes=[state, state],
        compiler_params=_params("parallel", "arbitrary"),
        name="hgrn2",
    )(proj3, proj3, proj3, lower_bounds)


def _lin_scan(a, u, carry, h_ref, reverse):
    n, w = a.shape
    n_groups = n // SUBLANES
    a = a.reshape(n_groups, SUBLANES, w)
    u = u.reshape(n_groups, SUBLANES, w)
    sub = lax.broadcasted_iota(jnp.int32, a.shape, 1)
    s = 1
    while s < SUBLANES:
        shift = (SUBLANES - s) if reverse else s
        m = (sub < SUBLANES - s) if reverse else (sub >= s)
        a_sh = jnp.where(m, pltpu.roll(a, shift, 1), 1.0)
        u_sh = jnp.where(m, pltpu.roll(u, shift, 1), 0.0)
        u = u + a * u_sh
        a = a * a_sh
        s *= 2
    groups = range(n_groups)
    for g in (reversed(groups) if reverse else groups):
        hg = u[g] + a[g] * carry
        h_ref[g * SUBLANES:(g + 1) * SUBLANES, :] = hg
        carry = hg[0:1] if reverse else hg[SUBLANES - 1:SUBLANES]
    return carry


def _griffin_dir(main_ref, prev_ref, next_ref, is_first, is_last, ext_ref, cw, cb, wg_ref, bias, sp,
                 carry_ref, h_ref, reverse):
    ts, w = main_ref.shape
    halo = SUBLANES
    ext_ref[0:halo, :] = jnp.where(is_first, 0.0, prev_ref[...])
    ext_ref[halo:halo + ts, :] = main_ref[...]
    ext_ref[halo + ts:2 * halo + ts, :] = jnp.where(is_last, 0.0, next_ref[...])
    xc = cb
    for j in range(CONV_WIDTH):
        xc = xc + cw[j:j + 1] * ext_ref[halo - 2 + j:halo - 2 + j + ts, :]

    half = w // 2
    pre = [_bdot(xc[:, i * half:(i + 1) * half], wg_ref[i]) for i in range(2)]
    r_pre = jnp.concatenate([pre[0][:, :half], pre[1][:, :half]], axis=1) + bias[0:1]
    i_pre = jnp.concatenate([pre[0][:, half:], pre[1][:, half:]], axis=1) + bias[1:2]
    r = jax.nn.sigmoid(r_pre)
    ig = jax.nn.sigmoid(i_pre)
    log_a = (-LRU_C) * r * sp
    a = jnp.exp(log_a)
    u = jnp.sqrt(jnp.maximum(1.0 - a * a, 0.0)) * (ig * xc)
    carry_ref[0:1, :] = _lin_scan(a, u, carry_ref[0:1, :], h_ref, reverse)


def _griffin_kernel(fm_ref, fp_ref, fn_ref, bm_ref, bp_ref, bn_ref, cw_ref, cb_ref, wg_ref, bias_ref, lam_ref,
                    hf_ref, hb_ref, ext_ref, cf_ref, cbk_ref):
    j = pl.program_id(1)
    ns = pl.num_programs(1)

    @pl.when(j == 0)
    def _():
        cf_ref[...] = jnp.zeros_like(cf_ref)
        cbk_ref[...] = jnp.zeros_like(cbk_ref)

    cw = cw_ref[...]
    cb = cb_ref[...]
    sp = jax.nn.softplus(-lam_ref[...])
    _griffin_dir(fm_ref, fp_ref, fn_ref, j == 0, j == ns - 1, ext_ref, cw, cb, wg_ref.at[0], bias_ref[0], sp[0:1],
                 cf_ref, hf_ref, False)
    _griffin_dir(bm_ref, bp_ref, bn_ref, j == ns - 1, j == 0, ext_ref, cw, cb, wg_ref.at[1], bias_ref[1], sp[1:2],
                 cbk_ref, hb_ref, True)


def _griffin(proj3, conv_w, conv_b3, wg, gate_bias, lam, layer, lru_width, col_block):
    bsz, s, _ = proj3.shape
    w = lru_width
    ts = min(256, s)
    ns = s // ts
    tb = ts // SUBLANES
    nb8 = s // SUBLANES
    out = jax.ShapeDtypeStruct((bsz, s, w), F32)
    halo = (None, SUBLANES, w)
    return pl.pallas_call(
        _griffin_kernel,
        grid=(bsz, ns),
        in_specs=[
            pl.BlockSpec((None, ts, w), lambda b, j: (b, j, col_block)),
            pl.BlockSpec(halo, lambda b, j: (b, jnp.maximum(j * tb - 1, 0), col_block)),
            pl.BlockSpec(halo, lambda b, j: (b, jnp.minimum((j + 1) * tb, nb8 - 1), col_block)),
            pl.BlockSpec((None, ts, w), lambda b, j: (b, ns - 1 - j, col_block)),
            pl.BlockSpec(halo, lambda b, j: (b, jnp.maximum((ns - 1 - j) * tb - 1, 0), col_block)),
            pl.BlockSpec(halo, lambda b, j: (b, jnp.minimum((ns - j) * tb, nb8 - 1), col_block)),
            pl.BlockSpec((None, CONV_WIDTH, w), lambda b, j: (layer, 0, 0)),
            pl.BlockSpec((None, 1, w), lambda b, j: (layer, 0, 0)),
            pl.BlockSpec((None, 2, 2, w // 2, w), lambda b, j: (layer, 0, 0, 0, 0)),
            pl.BlockSpec((None, 2, 2, w), lambda b, j: (layer, 0, 0, 0)),
            pl.BlockSpec((None, 2, w), lambda b, j: (layer, 0, 0)),
        ],
        out_specs=[
            pl.BlockSpec((None, ts, w), lambda b, j: (b, j, 0)),
            pl.BlockSpec((None, ts, w), lambda b, j: (b, ns - 1 - j, 0)),
        ],
        out_shape=[out, out],
        scratch_shapes=[
            pltpu.VMEM((ts + 2 * SUBLANES, w), F32),
            pltpu.VMEM((SUBLANES, w), F32),
            pltpu.VMEM((SUBLANES, w), F32),
        ],
        compiler_params=_params("parallel", "arbitrary"),
        name="griffin",
    )(proj3, proj3, proj3, proj3, proj3, proj3, conv_w, conv_b3, wg, gate_bias, lam)


def _gelu_tanh(x):
    return 0.5 * x * (1.0 + jnp.tanh(0.7978845608028654 * (x + 0.044715 * x * x * x)))


def _mix_kernel(x_ref, of_ref, ob_ref, zg_ref, hf_ref, hb_ref, zy_ref, wout_ref, hgn_ref, lrn_ref, g1_ref, b1_ref,
                rw_ref, rb_ref, tri_ref, x1_ref, ridx_ref, rgate_ref, counts_ref, cnt_ref, *, alpha):
    @pl.when(pl.program_id(0) == 0)
    def _():
        cnt_ref[...] = jnp.zeros_like(cnt_ref)

    o = of_ref[...] + ob_ref[...]
    w = o.shape[1]
    hd = w // HG_HEADS
    parts = []
    for h in range(HG_HEADS):
        oh = o[:, h * hd:(h + 1) * hd]
        parts.append(oh * lax.rsqrt(jnp.mean(oh * oh, axis=-1, keepdims=True) + RMS_EPS))
    zg = zg_ref[...]
    o_hg = jnp.concatenate(parts, axis=1) * hgn_ref[...] * (zg * jax.nn.sigmoid(zg))
    hh = hf_ref[...] + hb_ref[...]
    o_lru = hh * lax.rsqrt(jnp.mean(hh * hh, axis=-1, keepdims=True) + RMS_EPS) * lrn_ref[...]
    o_lru = o_lru * _gelu_tanh(zy_ref[...])
    y = _bdot(o_hg, wout_ref[0:w, :]) + _bdot(o_lru, wout_ref[w:, :])
    x1 = _layernorm(alpha * x_ref[...] + y, g1_ref[...], b1_ref[...])
    _store_row_tiles(x1_ref, x1)

    rw = rw_ref[...]
    x_hi, w_hi = x1.astype(BF16), rw.astype(BF16)
    x_lo = (x1 - x_hi.astype(F32)).astype(BF16)
    w_lo = (rw - w_hi.astype(F32)).astype(BF16)
    logits = (jnp.dot(x_hi, w_hi, preferred_element_type=F32) + jnp.dot(x_lo, w_hi, preferred_element_type=F32)
              + jnp.dot(x_hi, w_lo, preferred_element_type=F32) + rb_ref[...])
    n_exp = logits.shape[1]
    eid = lax.broadcasted_iota(jnp.int32, logits.shape, 1)
    vals, idxs = [], []
    for _ in range(TOP_K):
        m = jnp.max(logits, axis=-1, keepdims=True)
        idx = jnp.min(jnp.where(logits == m, eid, n_exp), axis=-1, keepdims=True)
        vals.append(m)
        idxs.append(idx)
        logits = jnp.where(eid == idx, -jnp.inf, logits)
    exps = [jnp.exp(v - vals[0]) for v in vals]
    denom = exps[0] + exps[1] + exps[2] + exps[3]

    base = cnt_ref[0:1, 0:n_exp]
    ranks = []
    for kk in range(TOP_K):
        onehot = (eid == idxs[kk]).astype(F32)
        before = jnp.dot(tri_ref[...], onehot.astype(BF16), preferred_element_type=F32) + base
        ranks.append(jnp.sum(onehot * before, axis=-1, keepdims=True).astype(jnp.int32))
        base = base + jnp.sum(onehot, axis=0, keepdims=True)
    cnt_ref[0:1, 0:n_exp] = base
    counts_ref[...] = cnt_ref[...]

    lane = lax.broadcasted_iota(jnp.int32, ridx_ref.shape, 1)
    ridx = jnp.zeros(ridx_ref.shape, jnp.int32)
    rgate = jnp.zeros(rgate_ref.shape, F32)
    for kk in range(TOP_K):
        ridx = jnp.where(lane == kk, idxs[kk], ridx)
        ridx = jnp.where(lane == TOP_K + kk, ranks[kk], ridx)
        rgate = jnp.where(lane == kk, exps[kk] / denom, rgate)
    ridx_ref[...] = ridx
    rgate_ref[...] = rgate


def _mix(x2, o_f, o_b, h_f, h_b, proj, w_out_bf16, hg_norm3, lru_norm3, ln_g3, ln_b3, router_w, router_b3, layer,
         alpha, zg_block, zy_block):
    t, d = x2.shape
    w = o_f.shape[1]
    n_exp = router_w.shape[-1]
    tm = min(512, t)
    row_blk = lambda width: pl.BlockSpec((tm, width), lambda i: (i, 0))
    vec = lambda width: pl.BlockSpec((None, 1, width), lambda i: (layer, 0, 0))
    return pl.pallas_call(
        functools.partial(_mix_kernel, alpha=alpha),
        grid=(t // tm,),
        in_specs=[
            row_blk(d), row_blk(w), row_blk(w),
            pl.BlockSpec((tm, w), lambda i: (i, zg_block)),
            row_blk(w), row_blk(w),
            pl.BlockSpec((tm, w), lambda i: (i, zy_block)),
            pl.BlockSpec((None, 2 * w, d), lambda i: (layer, 0, 0)),
            vec(w), vec(w), vec(d), vec(d),
            pl.BlockSpec((None, d, n_exp), lambda i: (layer, 0, 0)),
            vec(n_exp),
            pl.BlockSpec((tm, tm), lambda i: (0, 0)),
        ],
        out_specs=[pl.BlockSpec((tm * d // LANES, LANES), lambda i: (i, 0)), row_blk(LANES), row_blk(LANES),
                   pl.BlockSpec((SUBLANES, LANES), lambda i: (0, 0))],
        out_shape=[
            jax.ShapeDtypeStruct((t * d // LANES, LANES), F32),
            jax.ShapeDtypeStruct((t, LANES), jnp.int32),
            jax.ShapeDtypeStruct((t, LANES), F32),
            jax.ShapeDtypeStruct((SUBLANES, LANES), F32),
        ],
        scratch_shapes=[pltpu.VMEM((SUBLANES, LANES), F32)],
        compiler_params=_params("arbitrary"),
        name="mix_ln_router",
    )(x2, o_f, o_b, proj, h_f, h_b, proj, w_out_bf16, hg_norm3, lru_norm3, ln_g3, ln_b3, router_w, router_b3,
      jnp.tril(jnp.ones((tm, tm), BF16), k=-1))


DMA_UNROLL_ROWS = 32


def _routing_tables(ridx, counts_f32, n_exp, block_rows):
    t = ridx.shape[0]
    idx = ridx[:, :TOP_K]
    rank = ridx[:, TOP_K:2 * TOP_K]
    counts = counts_f32[0, :n_exp].astype(jnp.int32)
    padded = ((counts + block_rows - 1) // block_rows) * block_rows
    pends = jnp.cumsum(padded)
    pstarts = pends - padded
    onehot = idx[:, :, None] == jnp.arange(n_exp, dtype=jnp.int32)[None, None, :]
    dest = (jnp.sum(jnp.where(onehot, pstarts[None, None, :], 0), axis=-1) + rank).reshape(-1).astype(jnp.int32)
    n_rows = t * TOP_K + n_exp * block_rows
    n_blocks = n_rows // block_rows
    blk_start = jnp.arange(n_blocks, dtype=jnp.int32) * block_rows
    blk_e = jnp.minimum(jnp.searchsorted(pends, blk_start, side="right", method="compare_all"),
                        n_exp - 1).astype(jnp.int32)
    n_used = (pends[-1] // block_rows).astype(jnp.int32).reshape(1)
    return dest, counts, pstarts.astype(jnp.int32), blk_e, n_used, n_rows


def _dispatch_kernel(cnt_ref, pst_ref, dest_ref, x_ref, xs_ref, zero_ref, sem, *, block_rows):
    per_row = xs_ref.shape[1]
    tm = x_ref.shape[0] // per_row

    def issue(g, carry):
        r0 = pl.multiple_of(g * DMA_UNROLL_ROWS, DMA_UNROLL_ROWS)
        for j in range(DMA_UNROLL_ROWS):
            src = x_ref.at[pl.ds(pl.multiple_of((r0 + j) * per_row, per_row), per_row)]
            for kk in range(TOP_K):
                d = dest_ref[(r0 + j) * TOP_K + kk]
                pltpu.make_async_copy(src, xs_ref.at[d], sem).start(priority=kk % 2)
        return carry

    lax.fori_loop(0, tm // DMA_UNROLL_ROWS, issue, 0)
    for _ in range(TOP_K):
        pltpu.make_async_copy(xs_ref.at[pl.ds(0, tm)], xs_ref.at[pl.ds(0, tm)], sem).wait()

    @pl.when(pl.program_id(0) == pl.num_programs(0) - 1)
    def _():
        zero_ref[...] = jnp.zeros_like(zero_ref)
        n_exp = cnt_ref.shape[0]
        last = n_exp - 1
        cnt_last = cnt_ref[last]
        used_rows = pst_ref[last] + cnt_last + (block_rows - cnt_last % block_rows) % block_rows
        n_tail = (xs_ref.shape[0] - used_rows) // block_rows

        def tail_copy(b):
            start = pl.multiple_of(used_rows + b * block_rows, block_rows)
            return pltpu.make_async_copy(zero_ref, xs_ref.at[pl.ds(start, block_rows)], sem)

        def tail_start(b, c2):
            tail_copy(b).start()
            return c2

        def tail_wait(b, c2):
            tail_copy(b).wait()
            return c2

        lax.fori_loop(0, n_tail, tail_start, 0)
        lax.fori_loop(0, n_tail, tail_wait, 0)

        def per_expert(e, carry):
            cnt = cnt_ref[e]
            first = pst_ref[e] + cnt
            n_pad = (block_rows - cnt % block_rows) % block_rows

            def pieces(action):
                off = first
                for bit in range(block_rows.bit_length() - 1):
                    size = 1 << bit
                    copy = pltpu.make_async_copy(zero_ref.at[pl.ds(0, size)], xs_ref.at[pl.ds(off, size)], sem)
                    pl.when((n_pad & size) != 0)(functools.partial(action, copy))
                    off = off + (n_pad & size)

            pieces(lambda copy: copy.start())
            pieces(lambda copy: copy.wait())
            return carry

        lax.fori_loop(0, n_exp, per_expert, 0)


def _dispatch(x1_tiles, dest, counts, pstarts, n_rows, block_rows, d):
    per_row = d // LANES
    t = x1_tiles.shape[0] // per_row
    tm = min(512, t)
    assert tm % DMA_UNROLL_ROWS == 0 and block_rows & (block_rows - 1) == 0
    return pl.pallas_call(
        functools.partial(_dispatch_kernel, block_rows=block_rows),
        grid_spec=pltpu.PrefetchScalarGridSpec(
            num_scalar_prefetch=2,
            grid=(t // tm,),
            in_specs=[
                pl.BlockSpec((tm * TOP_K,), lambda i, c, p: (i,), memory_space=pltpu.SMEM),
                pl.BlockSpec((tm * per_row, LANES), lambda i, c, p: (i, 0)),
            ],
            out_specs=pl.BlockSpec(memory_space=pl.ANY),
            scratch_shapes=[pltpu.VMEM((block_rows, per_row, LANES), F32), pltpu.SemaphoreType.DMA],
        ),
        out_shape=jax.ShapeDtypeStruct((n_rows, per_row, LANES), F32),
        compiler_params=_params("arbitrary"),
        name="moe_dispatch",
    )(counts, pstarts, dest, x1_tiles)


def _moe_ffn_kernel(blk_e_ref, n_used_ref, xs_ref, wgu_f32_ref, bgu_ref, wdn_f32_ref, bdn_ref, ys_ref,
                    wgu_ref, wdn_ref):
    i = pl.program_id(0)
    used = i < n_used_ref[0]

    @pl.when(used & ((i == 0) | (blk_e_ref[i] != blk_e_ref[jnp.maximum(i - 1, 0)])))
    def _():
        rows = LANES

        def cast_gu(j, carry):
            sl = pl.ds(pl.multiple_of(j * rows, rows), rows)
            wgu_ref[sl, :] = wgu_f32_ref[sl, :].astype(BF16)
            return carry

        def cast_dn(j, carry):
            sl = pl.ds(pl.multiple_of(j * rows, rows), rows)
            wdn_ref[sl, :] = wdn_f32_ref[sl, :].astype(BF16)
            return carry

        lax.fori_loop(0, wgu_ref.shape[0] // rows, cast_gu, 0)
        lax.fori_loop(0, wdn_ref.shape[0] // rows, cast_dn, 0)

    @pl.when(used)
    def _():
        de = wdn_ref.shape[0]
        bm = xs_ref.shape[0] * LANES // wgu_ref.shape[0]
        x = _load_row_tiles(xs_ref, bm).astype(BF16)
        gu = jnp.dot(x, wgu_ref[...], preferred_element_type=F32) + bgu_ref[...]
        gate = jnp.minimum(gu[:, :de], SWIGLU_LIMIT)
        up = jnp.clip(gu[:, de:], -SWIGLU_LIMIT, SWIGLU_LIMIT)
        hid = (up + 1.0) * (gate * jax.nn.sigmoid(SWIGLU_ALPHA * gate))
        _store_row_tiles(ys_ref, jnp.dot(hid.astype(BF16), wdn_ref[...], preferred_element_type=F32) + bdn_ref[...])

    @pl.when(jnp.logical_not(used))
    def _():
        ys_ref[...] = jnp.zeros_like(ys_ref)


def _moe_ffn(xs_tiles, blk_e, n_used, w_gate_up, b_gu4, w_down, b_dn4, layer, block_rows):
    d, de = w_down.shape[3], w_down.shape[2]
    per_row = d // LANES
    n_blocks = xs_tiles.shape[0] // (block_rows * per_row)
    w_map = lambda i, be, nu: (layer, be[i], 0, 0)
    return pl.pallas_call(
        _moe_ffn_kernel,
        grid_spec=pltpu.PrefetchScalarGridSpec(
            num_scalar_prefetch=2,
            grid=(n_blocks,),
            in_specs=[
                pl.BlockSpec((block_rows * per_row, LANES), lambda i, be, nu: (jnp.minimum(i, nu[0] - 1), 0)),
                pl.BlockSpec((None, None, d, 2 * de), w_map),
                pl.BlockSpec((None, None, 1, 2 * de), w_map),
                pl.BlockSpec((None, None, de, d), w_map),
                pl.BlockSpec((None, None, 1, d), w_map),
            ],
            out_specs=pl.BlockSpec((block_rows * per_row, LANES), lambda i, be, nu: (i, 0)),
            scratch_shapes=[pltpu.VMEM((d, 2 * de), BF16), pltpu.VMEM((de, d), BF16)],
        ),
        out_shape=jax.ShapeDtypeStruct(xs_tiles.shape, F32),
        compiler_params=_params("arbitrary"),
        name="moe_ffn",
    )(blk_e, n_used, xs_tiles, w_gate_up, b_gu4, w_down, b_dn4)


def _combine_kernel(dest_ref, gate_ref, x1_ref, ys_ref, g2_ref, b2_ref, o_ref, buf0, buf1, sem, *, alpha):
    i = pl.program_id(0)
    n_tiles = pl.num_programs(0) - 1
    tm = o_ref.shape[0]
    per_row = ys_ref.shape[1]

    def issue(buf, slot):
        def body(g, carry):
            r0 = pl.multiple_of(g * DMA_UNROLL_ROWS, DMA_UNROLL_ROWS)
            for j in range(DMA_UNROLL_ROWS):
                rows = pl.ds(pl.multiple_of((r0 + j) * per_row, per_row), per_row)
                for kk in range(TOP_K):
                    src = dest_ref[(r0 + j) * TOP_K + kk]
                    pltpu.make_async_copy(ys_ref.at[src], buf.at[kk, rows], sem.at[slot]).start(priority=kk % 2)
            return carry

        lax.fori_loop(0, tm // DMA_UNROLL_ROWS, body, 0)

    def finish(buf, slot):
        for kk in range(TOP_K):
            pltpu.make_async_copy(ys_ref.at[pl.ds(0, tm)], ys_ref.at[pl.ds(0, tm)], sem.at[slot]).wait()
        gates = gate_ref[...]
        m = gates[:, 0:1] * _load_row_tiles(buf.at[0], tm)
        for kk in range(1, TOP_K):
            m = m + gates[:, kk:kk + 1] * _load_row_tiles(buf.at[kk], tm)
        o_ref[...] = _layernorm(alpha * _load_row_tiles(x1_ref, tm) + m, g2_ref[...], b2_ref[...])

    for parity, (cur, prev) in enumerate(((buf0, buf1), (buf1, buf0))):
        @pl.when(i % 2 == parity)
        def _():
            @pl.when(i < n_tiles)
            def _():
                issue(cur, parity)

            @pl.when(i > 0)
            def _():
                finish(prev, 1 - parity)


def _combine(x1_tiles, ys3, dest, rgate, ln_g3, ln_b3, layer, alpha):
    per_row = ys3.shape[1]
    d = per_row * LANES
    t = x1_tiles.shape[0] // per_row
    tm = min(256, t)
    nt = t // tm
    assert tm % DMA_UNROLL_ROWS == 0
    done = lambda i: (jnp.maximum(i - 1, 0), 0)
    row_buffer = pltpu.VMEM((TOP_K, tm * per_row, LANES), F32)
    return pl.pallas_call(
        functools.partial(_combine_kernel, alpha=alpha),
        grid=(nt + 1,),
        in_specs=[
            pl.BlockSpec((tm * TOP_K,), lambda i: (jnp.minimum(i, nt - 1),), memory_space=pltpu.SMEM),
            pl.BlockSpec((tm, LANES), done),
            pl.BlockSpec((tm * per_row, LANES), done),
            pl.BlockSpec(memory_space=pl.ANY),
            pl.BlockSpec((None, 1, d), lambda i: (layer, 0, 0)),
            pl.BlockSpec((None, 1, d), lambda i: (layer, 0, 0)),
        ],
        out_specs=pl.BlockSpec((tm, d), done),
        out_shape=jax.ShapeDtypeStruct((t, d), F32),
        scratch_shapes=[row_buffer, row_buffer, pltpu.SemaphoreType.DMA((2,))],
        compiler_params=_params("arbitrary"),
        name="moe_combine",
    )(dest, rgate, x1_tiles, ys3, ln_g3, ln_b3)


def _block_diag_gate_weights(wa, wx):
    n_l, n_dir, n_h, hd, _ = wa.shape
    hh = n_h // 2
    eye = jnp.eye(hh, dtype=wa.dtype)

    def bd(wm):
        wm = wm.reshape(n_l, n_dir, 2, hh, hd, hd)
        full = jnp.einsum("ldghij,hk->ldghikj", wm, eye)
        return full.reshape(n_l, n_dir, 2, hh * hd, hh * hd)

    return jnp.concatenate([bd(wa), bd(wx)], axis=-1).astype(BF16)


def kernel(x, w_in, hg_lb, hg_norm, lru_conv_w, lru_conv_b, lru_wa, lru_ba, lru_wx, lru_bx, lru_lambda, lru_norm,
           w_out, ln1_g, ln1_b, router_w, router_b, w_gate_up, b_gate_up, w_down, b_down, ln2_g, ln2_b):
    bsz, s, d = x.shape
    depth = w_in.shape[0]
    hg_w = hg_lb.shape[-1]
    lru_w = lru_lambda.shape[-1]
    n_exp = router_w.shape[-1]
    de = w_down.shape[2]
    t = bsz * s
    alpha = float((2 * depth) ** 0.25)
    block_rows = 2 * MXU_DIM
    assert w_in.shape[-1] == 5 * hg_w + 2 * lru_w and hg_w == lru_w
    assert s % 256 == 0 or s < 256

    p = jax.nn.softmax(hg_lb.astype(F32), axis=0)
    lower_bounds = jnp.clip(jnp.cumsum(p, axis=0) - p[0:1], 0.0, 1.0 - 1e-6)
    w_in_b = w_in.astype(BF16)
    w_out_b = w_out.astype(BF16)
    wg = _block_diag_gate_weights(lru_wa, lru_wx)
    gate_bias = jnp.stack([lru_ba, lru_bx], axis=2).astype(F32)
    row3 = lambda a: a.astype(F32).reshape(depth, 1, a.shape[-1])
    b_gu4 = b_gate_up.astype(F32).reshape(depth, n_exp, 1, 2 * de)
    b_dn4 = b_down.astype(F32).reshape(depth, n_exp, 1, d)

    x2 = x.reshape(t, d)
    for layer in range(depth):
        proj = _in_proj(x2, w_in_b, layer)
        proj3 = proj.reshape(bsz, s, proj.shape[-1])
        o_f, o_b = _hgrn2(proj3, lower_bounds, layer, hg_w)
        h_f, h_b = _griffin(proj3, lru_conv_w, row3(lru_conv_b), wg, gate_bias, lru_lambda, layer, lru_w,
                            col_block=5)
        x1, ridx, rgate, counts_f32 = _mix(x2, o_f.reshape(t, hg_w), o_b.reshape(t, hg_w), h_f.reshape(t, lru_w),
                               h_b.reshape(t, lru_w), proj, w_out_b, row3(hg_norm), row3(lru_norm), row3(ln1_g),
                               row3(ln1_b), router_w, row3(router_b), layer, alpha, zg_block=4, zy_block=6)
        dest, counts, pstarts, blk_e, n_used, n_rows = _routing_tables(ridx, counts_f32, n_exp, block_rows)
        xs = _dispatch(x1, dest, counts, pstarts, n_rows, block_rows, d)
        ys = _moe_ffn(xs.reshape(-1, LANES), blk_e, n_used, w_gate_up, b_gu4, w_down, b_dn4, layer, block_rows)
        x2 = _combine(x1, ys.reshape(xs.shape), dest, rgate, row3(ln2_g), row3(ln2_b), layer, alpha)
    return x2.reshape(bsz, s, d)
```

```python
import functools

import jax
import jax.numpy as jnp
from jax import lax
from jax.experimental import pallas as pl
from jax.experimental.pallas import tpu as pltpu

F32 = jnp.float32
BF16 = jnp.bfloat16

HG_HEADS = 4
HG_CHUNK = 32
LB_FLOOR = 1e-30
LRU_HEADS = 8
LRU_C = 8.0
CONV_WIDTH = 4
TOP_K = 4
SWIGLU_LIMIT = 7.0
SWIGLU_ALPHA = 1.702
LN_EPS = 1e-5
RMS_EPS = 1e-6

LANES = 128
SUBLANES = 8
MXU_DIM = 256
VMEM_LIMIT_BYTES = 56 * 1024 * 1024

NT_DIMS = (((1,), (1,)), ((), ()))
TN_DIMS = (((0,), (0,)), ((), ()))


def _params(*semantics):
    return pltpu.CompilerParams(dimension_semantics=semantics, vmem_limit_bytes=VMEM_LIMIT_BYTES)


def _bdot(a, b):
    return jnp.dot(a.astype(BF16), b.astype(BF16), preferred_element_type=F32)


def _bdot_general(a, b, dims):
    return lax.dot_general(a.astype(BF16), b.astype(BF16), dims, preferred_element_type=F32)


def _layernorm(t, g, b):
    mu = jnp.mean(t, axis=-1, keepdims=True)
    c = t - mu
    var = jnp.mean(c * c, axis=-1, keepdims=True)
    return c * lax.rsqrt(var + LN_EPS) * g + b


def _load_row_tiles(ref, n_rows):
    per_row = ref.shape[0] // n_rows
    return jnp.concatenate([ref[pl.ds(j, n_rows, stride=per_row), :] for j in range(per_row)], axis=1)


def _store_row_tiles(ref, value):
    n_rows, d = value.shape
    per_row = d // LANES
    for j in range(per_row):
        ref[pl.ds(j, n_rows, stride=per_row), :] = value[:, j * LANES:(j + 1) * LANES]


def _in_proj_kernel(x_ref, w_ref, o_ref):
    o_ref[...] = jnp.dot(x_ref[...].astype(BF16), w_ref[...], preferred_element_type=F32)


def _in_proj(x2, w_in_bf16, layer):
    t, d = x2.shape
    n = w_in_bf16.shape[-1]
    tm = min(512, t)
    return pl.pallas_call(
        _in_proj_kernel,
        grid=(t // tm,),
        in_specs=[
            pl.BlockSpec((tm, d), lambda i: (i, 0)),
            pl.BlockSpec((None, d, n), lambda i: (layer, 0, 0)),
        ],
        out_specs=pl.BlockSpec((tm, n), lambda i: (i, 0)),
        out_shape=jax.ShapeDtypeStruct((t, n), F32),
        compiler_params=_params("parallel"),
        name="in_proj",
    )(x2, w_in_bf16)


def _cumsum_rows(x):
    n = x.shape[0]
    row = lax.broadcasted_iota(jnp.int32, x.shape, 0)
    s = 1
    while s < n:
        x = x + jnp.where(row >= s, pltpu.roll(x, s, 0), 0.0)
        s *= 2
    return x


def _hg_chunk(zq, zv, zf, lb, log_lb, log1m_lb, st_ref, o_ref, r0, reverse):
    c = HG_CHUNK
    q = zq * jax.nn.sigmoid(zq)
    e = jnp.exp(-jnp.abs(zf))
    log_sig = jnp.minimum(zf, 0.0) - jnp.log1p(e)
    sig_neg = jnp.where(zf >= 0.0, e, 1.0) / (1.0 + e)
    b = log1m_lb + log_sig
    logf = jnp.maximum(log_lb, b) + jnp.log1p(jnp.exp(-jnp.abs(log_lb - b)))
    k = (1.0 - lb) * sig_neg

    p = _cumsum_rows(logf)
    g_last = p[c - 1:c]
    if reverse:
        g = g_last - p + logf
        g_ref = g[c // 2:c // 2 + 1]
    else:
        g = p
        g_ref = g[c // 2 - 1:c // 2]
    qg = q * jnp.exp(g - g_ref)
    kg = k * jnp.exp(g_ref - g)
    kl = kg * jnp.exp(g_last - g_ref)
    qe = qg * jnp.exp(g_ref)
    dec = jnp.exp(g_last)

    row = lax.broadcasted_iota(jnp.int32, (c, c), 0)
    col = lax.broadcasted_iota(jnp.int32, (c, c), 1)
    keep = (col >= row) if reverse else (col <= row)
    hd = zq.shape[1] // HG_HEADS
    for h in range(HG_HEADS):
        sl = slice(h * hd, (h + 1) * hd)
        scores = jnp.where(keep, _bdot_general(qg[:, sl], kg[:, sl], NT_DIMS), 0.0)
        st = st_ref[h]
        o = _bdot(scores, zv[:, sl]) + _bdot_general(qe[:, sl], st, NT_DIMS)
        o_ref[pl.ds(r0, c), sl] = o
        st_ref[h] = st * dec[:, sl] + _bdot_general(zv[:, sl], kl[:, sl], TN_DIMS)


def _hgrn2_kernel(fqif_ref, bqi_ref, bzb_ref, lb_ref, of_ref, ob_ref, stf_ref, stb_ref):
    @pl.when(pl.program_id(1) == 0)
    def _():
        stf_ref[...] = jnp.zeros_like(stf_ref)
        stb_ref[...] = jnp.zeros_like(stb_ref)

    w = lb_ref.shape[1]
    lb = lb_ref[...]
    log_lb = jnp.log(jnp.maximum(lb, LB_FLOOR))
    log1m_lb = jnp.log1p(-lb)
    n_chunks = of_ref.shape[0] // HG_CHUNK

    def body(ci, carry):
        r0 = pl.multiple_of(ci * HG_CHUNK, HG_CHUNK)
        rows = pl.ds(r0, HG_CHUNK)
        _hg_chunk(fqif_ref[rows, 0:w], fqif_ref[rows, w:2 * w], fqif_ref[rows, 2 * w:3 * w],
                  lb[0:1], log_lb[0:1], log1m_lb[0:1], stf_ref, of_ref, r0, False)
        rb = pl.multiple_of((n_chunks - 1 - ci) * HG_CHUNK, HG_CHUNK)
        rows_b = pl.ds(rb, HG_CHUNK)
        _hg_chunk(bqi_ref[rows_b, 0:w], bqi_ref[rows_b, w:2 * w], bzb_ref[rows_b, :],
                  lb[1:2], log_lb[1:2], log1m_lb[1:2], stb_ref, ob_ref, rb, True)
        return carry

    lax.fori_loop(0, n_chunks, body, 0, unroll=4)


def _hgrn2(proj3, lower_bounds, layer, hg_width):
    bsz, s, _ = proj3.shape
    w = hg_width
    ts = min(256, s)
    ns = s // ts
    hd = w // HG_HEADS
    out = jax.ShapeDtypeStruct((bsz, s, w), F32)
    state = pltpu.VMEM((HG_HEADS, hd, hd), F32)
    return pl.pallas_call(
        _hgrn2_kernel,
        grid=(bsz, ns),
        in_specs=[
            pl.BlockSpec((None, ts, 3 * w), lambda b, j: (b, j, 0)),
            pl.BlockSpec((None, ts, 2 * w), lambda b, j: (b, ns - 1 - j, 0)),
            pl.BlockSpec((None, ts, w), lambda b, j: (b, ns - 1 - j, 3)),
            pl.BlockSpec((None, 2, w), lambda b, j: (layer, 0, 0)),
        ],
        out_specs=[
            pl.BlockSpec((None, ts, w), lambda b, j: (b, j, 0)),
            pl.BlockSpec((None, ts, w), lambda b, j: (b, ns - 1 - j, 0)),
        ],
        out_shape=[out, out],
        scratch_shapes=[state, state],
        compiler_params=_params("parallel", "arbitrary"),
        name="hgrn2",
    )(proj3, proj3, proj3, lower_bounds)


def _lin_scan(a, u, carry, h_ref, reverse):
    n, w = a.shape
    n_groups = n // SUBLANES
    a = a.reshape(n_groups, SUBLANES, w)
    u = u.reshape(n_groups, SUBLANES, w)
    sub = lax.broadcasted_iota(jnp.int32, a.shape, 1)
    s = 1
    while s < SUBLANES:
        shift = (SUBLANES - s) if reverse else s
        m = (sub < SUBLANES - s) if reverse else (sub >= s)
        a_sh = jnp.where(m, pltpu.roll(a, shift, 1), 1.0)
        u_sh = jnp.where(m, pltpu.roll(u, shift, 1), 0.0)
        u = u + a * u_sh
        a = a * a_sh
        s *= 2
    groups = range(n_groups)
    for g in (reversed(groups) if reverse else groups):
        hg = u[g] + a[g] * carry
        h_ref[g * SUBLANES:(g + 1) * SUBLANES, :] = hg
        carry = hg[0:1] if reverse else hg[SUBLANES - 1:SUBLANES]
    return carry


def _griffin_dir(main_ref, prev_ref, next_ref, is_first, is_last, cw, cb, wg_ref, bias, sp,
                 carry_ref, h_ref, reverse):
    ts, w = main_ref.shape
    ng = ts // SUBLANES
    groups = jnp.concatenate([jnp.where(is_first, 0.0, prev_ref[...]).reshape(1, SUBLANES, w),
                              main_ref[...].reshape(ng, SUBLANES, w),
                              jnp.where(is_last, 0.0, next_ref[...]).reshape(1, SUBLANES, w)], axis=0)
    sub = lax.broadcasted_iota(jnp.int32, (ng, SUBLANES, w), 1)
    down1 = pltpu.roll(groups, 1, 1)
    down2 = pltpu.roll(groups, 2, 1)
    up1 = pltpu.roll(groups, SUBLANES - 1, 1)
    x_m1 = jnp.where(sub >= 1, down1[1:ng + 1], down1[0:ng])
    x_m2 = jnp.where(sub >= 2, down2[1:ng + 1], down2[0:ng])
    x_p1 = jnp.where(sub < SUBLANES - 1, up1[1:ng + 1], up1[2:ng + 2])
    taps = (x_m2, x_m1, groups[1:ng + 1], x_p1)
    xc = cb.reshape(1, 1, w)
    for j in range(CONV_WIDTH):
        xc = xc + cw[j:j + 1].reshape(1, 1, w) * taps[j]
    xc = xc.reshape(ts, w)

    half = w // 2
    pre = [_bdot(xc[:, i * half:(i + 1) * half], wg_ref[i]) for i in range(2)]
    r_pre = jnp.concatenate([pre[0][:, :half], pre[1][:, :half]], axis=1) + bias[0:1]
    i_pre = jnp.concatenate([pre[0][:, half:], pre[1][:, half:]], axis=1) + bias[1:2]
    r = 0.5 + 0.5 * jnp.tanh(0.5 * r_pre)
    ig = 0.5 + 0.5 * jnp.tanh(0.5 * i_pre)
    log_a = (-LRU_C) * r * sp
    a = jnp.exp(log_a)
    u = jnp.sqrt(jnp.maximum(1.0 - a * a, 0.0)) * (ig * xc)
    carry_ref[0:1, :] = _lin_scan(a, u, carry_ref[0:1, :], h_ref, reverse)


def _griffin_kernel(fm_ref, fp_ref, fn_ref, bm_ref, bp_ref, bn_ref, cw_ref, cb_ref, wg_ref, bias_ref, lam_ref,
                    hf_ref, hb_ref, cf_ref, cbk_ref):
    j = pl.program_id(1)
    ns = pl.num_programs(1)

    @pl.when(j == 0)
    def _():
        cf_ref[...] = jnp.zeros_like(cf_ref)
        cbk_ref[...] = jnp.zeros_like(cbk_ref)

    cw = cw_ref[...]
    cb = cb_ref[...]
    sp = jax.nn.softplus(-lam_ref[...])
    _griffin_dir(fm_ref, fp_ref, fn_ref, j == 0, j == ns - 1, cw, cb, wg_ref.at[0], bias_ref[0], sp[0:1],
                 cf_ref, hf_ref, False)
    _griffin_dir(bm_ref, bp_ref, bn_ref, j == ns - 1, j == 0, cw, cb, wg_ref.at[1], bias_ref[1], sp[1:2],
                 cbk_ref, hb_ref, True)


def _griffin(proj3, conv_w, conv_b3, wg, gate_bias, lam, layer, lru_width, col_block):
    bsz, s, _ = proj3.shape
    w = lru_width
    ts = min(256, s)
    ns = s // ts
    tb = ts // SUBLANES
    nb8 = s // SUBLANES
    out = jax.ShapeDtypeStruct((bsz, s, w), F32)
    halo = (None, SUBLANES, w)
    return pl.pallas_call(
        _griffin_kernel,
        grid=(bsz, ns),
        in_specs=[
            pl.BlockSpec((None, ts, w), lambda b, j: (b, j, col_block)),
            pl.BlockSpec(halo, lambda b, j: (b, jnp.maximum(j * tb - 1, 0), col_block)),
            pl.BlockSpec(halo, lambda b, j: (b, jnp.minimum((j + 1) * tb, nb8 - 1), col_block)),
            pl.BlockSpec((None, ts, w), lambda b, j: (b, ns - 1 - j, col_block)),
            pl.BlockSpec(halo, lambda b, j: (b, jnp.maximum((ns - 1 - j) * tb - 1, 0), col_block)),
            pl.BlockSpec(halo, lambda b, j: (b, jnp.minimum((ns - j) * tb, nb8 - 1), col_block)),
            pl.BlockSpec((None, CONV_WIDTH, w), lambda b, j: (layer, 0, 0)),
            pl.BlockSpec((None, 1, w), lambda b, j: (layer, 0, 0)),
            pl.BlockSpec((None, 2, 2, w // 2, w), lambda b, j: (layer, 0, 0, 0, 0)),
            pl.BlockSpec((None, 2, 2, w), lambda b, j: (layer, 0, 0, 0)),
            pl.BlockSpec((None, 2, w), lambda b, j: (layer, 0, 0)),
        ],
        out_specs=[
            pl.BlockSpec((None, ts, w), lambda b, j: (b, j, 0)),
            pl.BlockSpec((None, ts, w), lambda b, j: (b, ns - 1 - j, 0)),
        ],
        out_shape=[out, out],
        scratch_shapes=[
            pltpu.VMEM((SUBLANES, w), F32),
            pltpu.VMEM((SUBLANES, w), F32),
        ],
        compiler_params=_params("parallel", "arbitrary"),
        name="griffin",
    )(proj3, proj3, proj3, proj3, proj3, proj3, conv_w, conv_b3, wg, gate_bias, lam)


def _gelu_tanh(x):
    return 0.5 * x * (1.0 + jnp.tanh(0.7978845608028654 * (x + 0.044715 * x * x * x)))


def _mix_kernel(x_ref, of_ref, ob_ref, zg_ref, hf_ref, hb_ref, zy_ref, wout_ref, hgn_ref, lrn_ref, g1_ref, b1_ref,
                rw_ref, rb_ref, tri_ref, x1_ref, ridx_ref, rgate_ref, counts_ref, cnt_ref, *, alpha, n_exp):
    @pl.when(pl.program_id(0) == 0)
    def _():
        cnt_ref[...] = jnp.zeros_like(cnt_ref)

    o = of_ref[...] + ob_ref[...]
    w = o.shape[1]
    hd = w // HG_HEADS
    parts = []
    for h in range(HG_HEADS):
        oh = o[:, h * hd:(h + 1) * hd]
        parts.append(oh * lax.rsqrt(jnp.mean(oh * oh, axis=-1, keepdims=True) + RMS_EPS))
    zg = zg_ref[...]
    o_hg = jnp.concatenate(parts, axis=1) * hgn_ref[...] * (zg * jax.nn.sigmoid(zg))
    hh = hf_ref[...] + hb_ref[...]
    o_lru = hh * lax.rsqrt(jnp.mean(hh * hh, axis=-1, keepdims=True) + RMS_EPS) * lrn_ref[...]
    o_lru = o_lru * _gelu_tanh(zy_ref[...])
    y = _bdot(o_hg, wout_ref[0:w, :]) + _bdot(o_lru, wout_ref[w:, :])
    x1 = _layernorm(alpha * x_ref[...] + y, g1_ref[...], b1_ref[...])
    _store_row_tiles(x1_ref, x1)

    rw = rw_ref[...]
    x_hi, w_hi = x1.astype(BF16), rw.astype(BF16)
    x_lo = (x1 - x_hi.astype(F32)).astype(BF16)
    w_lo = (rw - w_hi.astype(F32)).astype(BF16)
    n_pad = rw.shape[1]
    hi_both = jnp.dot(x_hi, jnp.concatenate([w_hi, w_lo], axis=1), preferred_element_type=F32)
    logits = (hi_both[:, :n_pad] + hi_both[:, n_pad:] + jnp.dot(x_lo, w_hi, preferred_element_type=F32)
              + rb_ref[...])

    lt = jnp.transpose(logits)[0:n_exp]
    eid = lax.broadcasted_iota(jnp.int32, lt.shape, 0)
    vals, idxs = [], []
    for _ in range(TOP_K):
        m = jnp.max(lt, axis=0, keepdims=True)
        idx = jnp.min(jnp.where(lt == m, eid, n_exp), axis=0, keepdims=True)
        vals.append(m)
        idxs.append(idx)
        lt = jnp.where(eid == idx, -jnp.inf, lt)
    exps = [jnp.exp(v - vals[0]) for v in vals]
    denom = exps[0] + exps[1] + exps[2] + exps[3]

    base = cnt_ref[:, 0:1]
    ranks = []
    for kk in range(TOP_K):
        onehot = (eid == idxs[kk]).astype(F32)
        before = jnp.dot(onehot.astype(BF16), tri_ref[...], preferred_element_type=F32) + base
        ranks.append(jnp.sum(onehot * before, axis=0, keepdims=True).astype(jnp.int32))
        base = base + jnp.sum(onehot, axis=1, keepdims=True)
    cnt_ref[...] = jnp.broadcast_to(base, cnt_ref.shape)
    counts_ref[...] = cnt_ref[...]

    out_row = lax.broadcasted_iota(jnp.int32, ridx_ref.shape, 0)
    ridx = jnp.zeros(ridx_ref.shape, jnp.int32)
    rgate = jnp.zeros(rgate_ref.shape, F32)
    for kk in range(TOP_K):
        ridx = jnp.where(out_row == kk, idxs[kk], ridx)
        ridx = jnp.where(out_row == TOP_K + kk, ranks[kk], ridx)
        rgate = jnp.where(out_row == kk, exps[kk] / denom, rgate)
    ridx_ref[...] = ridx
    rgate_ref[...] = rgate


def _mix(x2, o_f, o_b, h_f, h_b, proj, w_out_bf16, hg_norm3, lru_norm3, ln_g3, ln_b3, router_w_pad, router_b3_pad,
         n_exp, layer, alpha, zg_block, zy_block):
    t, d = x2.shape
    w = o_f.shape[1]
    assert router_w_pad.shape[-1] == LANES and n_exp <= LANES and n_exp % SUBLANES == 0
    tm = min(512, t)
    row_blk = lambda width: pl.BlockSpec((tm, width), lambda i: (i, 0))
    col_blk = pl.BlockSpec((2 * TOP_K, tm), lambda i: (0, i))
    vec = lambda width: pl.BlockSpec((None, 1, width), lambda i: (layer, 0, 0))
    return pl.pallas_call(
        functools.partial(_mix_kernel, alpha=alpha, n_exp=n_exp),
        grid=(t // tm,),
        in_specs=[
            row_blk(d), row_blk(w), row_blk(w),
            pl.BlockSpec((tm, w), lambda i: (i, zg_block)),
            row_blk(w), row_blk(w),
            pl.BlockSpec((tm, w), lambda i: (i, zy_block)),
            pl.BlockSpec((None, 2 * w, d), lambda i: (layer, 0, 0)),
            vec(w), vec(w), vec(d), vec(d),
            pl.BlockSpec((None, d, LANES), lambda i: (layer, 0, 0)),
            vec(LANES),
            pl.BlockSpec((tm, tm), lambda i: (0, 0)),
        ],
        out_specs=[pl.BlockSpec((tm * d // LANES, LANES), lambda i: (i, 0)), col_blk, col_blk,
                   pl.BlockSpec((n_exp, LANES), lambda i: (0, 0))],
        out_shape=[
            jax.ShapeDtypeStruct((t * d // LANES, LANES), F32),
            jax.ShapeDtypeStruct((2 * TOP_K, t), jnp.int32),
            jax.ShapeDtypeStruct((2 * TOP_K, t), F32),
            jax.ShapeDtypeStruct((n_exp, LANES), F32),
        ],
        scratch_shapes=[pltpu.VMEM((n_exp, LANES), F32)],
        compiler_params=_params("arbitrary"),
        name="mix_ln_router",
    )(x2, o_f, o_b, proj, h_f, h_b, proj, w_out_bf16, hg_norm3, lru_norm3, ln_g3, ln_b3, router_w_pad,
      router_b3_pad, jnp.triu(jnp.ones((tm, tm), BF16), k=1))


DMA_UNROLL_ROWS = 32


def _routing_tables(ridx, counts_f32, n_exp, block_rows):
    t = ridx.shape[1]
    idx = ridx[:TOP_K].T
    rank = ridx[TOP_K:2 * TOP_K].T
    counts = counts_f32[:, 0].astype(jnp.int32)
    padded = ((counts + block_rows - 1) // block_rows) * block_rows
    pends = jnp.cumsum(padded)
    pstarts = pends - padded
    onehot = idx[:, :, None] == jnp.arange(n_exp, dtype=jnp.int32)[None, None, :]
    dest = (jnp.sum(jnp.where(onehot, pstarts[None, None, :], 0), axis=-1) + rank).reshape(-1).astype(jnp.int32)
    n_rows = t * TOP_K + n_exp * block_rows
    n_blocks = n_rows // block_rows
    blk_start = jnp.arange(n_blocks, dtype=jnp.int32) * block_rows
    blk_e = jnp.minimum(jnp.searchsorted(pends, blk_start, side="right", method="compare_all"),
                        n_exp - 1).astype(jnp.int32)
    n_used = (pends[-1] // block_rows).astype(jnp.int32).reshape(1)
    return dest, counts, pstarts.astype(jnp.int32), blk_e, n_used, n_rows


def _dispatch_kernel(cnt_ref, pst_ref, dest_ref, x_ref, xs_ref, zero_ref, sem, *, block_rows):
    per_row = xs_ref.shape[1]
    tm = x_ref.shape[0] // per_row

    def issue(g, carry):
        r0 = pl.multiple_of(g * DMA_UNROLL_ROWS, DMA_UNROLL_ROWS)
        for j in range(DMA_UNROLL_ROWS):
            src = x_ref.at[pl.ds(pl.multiple_of((r0 + j) * per_row, per_row), per_row)]
            for kk in range(TOP_K):
                d = dest_ref[(r0 + j) * TOP_K + kk]
                pltpu.make_async_copy(src, xs_ref.at[d], sem).start(priority=kk % 2)
        return carry

    lax.fori_loop(0, tm // DMA_UNROLL_ROWS, issue, 0)
    for _ in range(TOP_K):
        pltpu.make_async_copy(xs_ref.at[pl.ds(0, tm)], xs_ref.at[pl.ds(0, tm)], sem).wait()

    @pl.when(pl.program_id(0) == pl.num_programs(0) - 1)
    def _():
        zero_ref[...] = jnp.zeros_like(zero_ref)
        n_exp = cnt_ref.shape[0]
        last = n_exp - 1
        cnt_last = cnt_ref[last]
        used_rows = pst_ref[last] + cnt_last + (block_rows - cnt_last % block_rows) % block_rows
        n_tail = (xs_ref.shape[0] - used_rows) // block_rows

        def tail_copy(b):
            start = pl.multiple_of(used_rows + b * block_rows, block_rows)
            return pltpu.make_async_copy(zero_ref, xs_ref.at[pl.ds(start, block_rows)], sem)

        def tail_start(b, c2):
            tail_copy(b).start()
            return c2

        def tail_wait(b, c2):
            tail_copy(b).wait()
            return c2

        lax.fori_loop(0, n_tail, tail_start, 0)
        lax.fori_loop(0, n_tail, tail_wait, 0)

        def per_expert(e, carry):
            cnt = cnt_ref[e]
            first = pst_ref[e] + cnt
            n_pad = (block_rows - cnt % block_rows) % block_rows

            def pieces(action):
                off = first
                for bit in range(block_rows.bit_length() - 1):
                    size = 1 << bit
                    copy = pltpu.make_async_copy(zero_ref.at[pl.ds(0, size)], xs_ref.at[pl.ds(off, size)], sem)
                    pl.when((n_pad & size) != 0)(functools.partial(action, copy))
                    off = off + (n_pad & size)

            pieces(lambda copy: copy.start())
            pieces(lambda copy: copy.wait())
            return carry

        lax.fori_loop(0, n_exp, per_expert, 0)


def _dispatch(x1_tiles, dest, counts, pstarts, n_rows, block_rows, d):
    per_row = d // LANES
    t = x1_tiles.shape[0] // per_row
    tm = min(512, t)
    assert tm % DMA_UNROLL_ROWS == 0 and block_rows & (block_rows - 1) == 0
    return pl.pallas_call(
        functools.partial(_dispatch_kernel, block_rows=block_rows),
        grid_spec=pltpu.PrefetchScalarGridSpec(
            num_scalar_prefetch=2,
            grid=(t // tm,),
            in_specs=[
                pl.BlockSpec((tm * TOP_K,), lambda i, c, p: (i,), memory_space=pltpu.SMEM),
                pl.BlockSpec((tm * per_row, LANES), lambda i, c, p: (i, 0)),
            ],
            out_specs=pl.BlockSpec(memory_space=pl.ANY),
            scratch_shapes=[pltpu.VMEM((block_rows, per_row, LANES), F32), pltpu.SemaphoreType.DMA],
        ),
        out_shape=jax.ShapeDtypeStruct((n_rows, per_row, LANES), F32),
        compiler_params=_params("arbitrary"),
        name="moe_dispatch",
    )(counts, pstarts, dest, x1_tiles)


def _moe_ffn_kernel(blk_e_ref, n_used_ref, xs_ref, wgu_f32_ref, bgu_ref, wdn_f32_ref, bdn_ref, ys_ref,
                    wgu_ref, wdn_ref):
    i = pl.program_id(0)
    used = i < n_used_ref[0]

    @pl.when(used & ((i == 0) | (blk_e_ref[i] != blk_e_ref[jnp.maximum(i - 1, 0)])))
    def _():
        rows = LANES

        def cast_gu(j, carry):
            sl = pl.ds(pl.multiple_of(j * rows, rows), rows)
            wgu_ref[sl, :] = wgu_f32_ref[sl, :].astype(BF16)
            return carry

        def cast_dn(j, carry):
            sl = pl.ds(pl.multiple_of(j * rows, rows), rows)
            wdn_ref[sl, :] = wdn_f32_ref[sl, :].astype(BF16)
            return carry

        lax.fori_loop(0, wgu_ref.shape[0] // rows, cast_gu, 0)
        lax.fori_loop(0, wdn_ref.shape[0] // rows, cast_dn, 0)

    @pl.when(used)
    def _():
        de = wdn_ref.shape[0]
        bm = xs_ref.shape[0] * LANES // wgu_ref.shape[0]
        x = _load_row_tiles(xs_ref, bm).astype(BF16)
        gu = jnp.dot(x, wgu_ref[...], preferred_element_type=F32) + bgu_ref[...]
        gate = jnp.minimum(gu[:, :de], SWIGLU_LIMIT)
        up = jnp.clip(gu[:, de:], -SWIGLU_LIMIT, SWIGLU_LIMIT)
        hid = (up + 1.0) * (gate * jax.nn.sigmoid(SWIGLU_ALPHA * gate))
        _store_row_tiles(ys_ref, jnp.dot(hid.astype(BF16), wdn_ref[...], preferred_element_type=F32) + bdn_ref[...])

    @pl.when(jnp.logical_not(used))
    def _():
        ys_ref[...] = jnp.zeros_like(ys_ref)


def _moe_ffn(xs_tiles, blk_e, n_used, w_gate_up, b_gu4, w_down, b_dn4, layer, block_rows):
    d, de = w_down.shape[3], w_down.shape[2]
    per_row = d // LANES
    n_blocks = xs_tiles.shape[0] // (block_rows * per_row)
    w_map = lambda i, be, nu: (layer, be[i], 0, 0)
    return pl.pallas_call(
        _moe_ffn_kernel,
        grid_spec=pltpu.PrefetchScalarGridSpec(
            num_scalar_prefetch=2,
            grid=(n_blocks,),
            in_specs=[
                pl.BlockSpec((block_rows * per_row, LANES), lambda i, be, nu: (jnp.minimum(i, nu[0] - 1), 0)),
                pl.BlockSpec((None, None, d, 2 * de), w_map),
                pl.BlockSpec((None, None, 1, 2 * de), w_map),
                pl.BlockSpec((None, None, de, d), w_map),
                pl.BlockSpec((None, None, 1, d), w_map),
            ],
            out_specs=pl.BlockSpec((block_rows * per_row, LANES), lambda i, be, nu: (i, 0)),
            scratch_shapes=[pltpu.VMEM((d, 2 * de), BF16), pltpu.VMEM((de, d), BF16)],
        ),
        out_shape=jax.ShapeDtypeStruct(xs_tiles.shape, F32),
        compiler_params=_params("arbitrary"),
        name="moe_ffn",
    )(blk_e, n_used, xs_tiles, w_gate_up, b_gu4, w_down, b_dn4)


def _combine_kernel(dest_ref, gate_ref, x1_ref, ys_ref, g2_ref, b2_ref, o_ref, buf0, buf1, sem, *, alpha):
    i = pl.program_id(0)
    n_tiles = pl.num_programs(0) - 1
    tm = o_ref.shape[0]
    per_row = ys_ref.shape[1]

    def issue(buf, slot):
        def body(g, carry):
            r0 = pl.multiple_of(g * DMA_UNROLL_ROWS, DMA_UNROLL_ROWS)
            for j in range(DMA_UNROLL_ROWS):
                rows = pl.ds(pl.multiple_of((r0 + j) * per_row, per_row), per_row)
                for kk in range(TOP_K):
                    src = dest_ref[(r0 + j) * TOP_K + kk]
                    pltpu.make_async_copy(ys_ref.at[src], buf.at[kk, rows], sem.at[slot]).start(priority=kk % 2)
            return carry

        lax.fori_loop(0, tm // DMA_UNROLL_ROWS, body, 0)

    def finish(buf, slot):
        for kk in range(TOP_K):
            pltpu.make_async_copy(ys_ref.at[pl.ds(0, tm)], ys_ref.at[pl.ds(0, tm)], sem.at[slot]).wait()
        gates = gate_ref[...]
        m = gates[:, 0:1] * _load_row_tiles(buf.at[0], tm)
        for kk in range(1, TOP_K):
            m = m + gates[:, kk:kk + 1] * _load_row_tiles(buf.at[kk], tm)
        o_ref[...] = _layernorm(alpha * _load_row_tiles(x1_ref, tm) + m, g2_ref[...], b2_ref[...])

    for parity, (cur, prev) in enumerate(((buf0, buf1), (buf1, buf0))):
        @pl.when(i % 2 == parity)
        def _():
            @pl.when(i < n_tiles)
            def _():
                issue(cur, parity)

            @pl.when(i > 0)
            def _():
                finish(prev, 1 - parity)


def _combine(x1_tiles, ys3, dest, rgate, ln_g3, ln_b3, layer, alpha):
    per_row = ys3.shape[1]
    d = per_row * LANES
    t = x1_tiles.shape[0] // per_row
    tm = min(512, t)
    nt = t // tm
    assert tm % DMA_UNROLL_ROWS == 0
    done = lambda i: (jnp.maximum(i - 1, 0), 0)
    row_buffer = pltpu.VMEM((TOP_K, tm * per_row, LANES), F32)
    return pl.pallas_call(
        functools.partial(_combine_kernel, alpha=alpha),
        grid=(nt + 1,),
        in_specs=[
            pl.BlockSpec((tm * TOP_K,), lambda i: (jnp.minimum(i, nt - 1),), memory_space=pltpu.SMEM),
            pl.BlockSpec((tm, rgate.shape[1]), done),
            pl.BlockSpec((tm * per_row, LANES), done),
            pl.BlockSpec(memory_space=pl.ANY),
            pl.BlockSpec((None, 1, d), lambda i: (layer, 0, 0)),
            pl.BlockSpec((None, 1, d), lambda i: (layer, 0, 0)),
        ],
        out_specs=pl.BlockSpec((tm, d), done),
        out_shape=jax.ShapeDtypeStruct((t, d), F32),
        scratch_shapes=[row_buffer, row_buffer, pltpu.SemaphoreType.DMA((2,))],
        compiler_params=_params("arbitrary"),
        name="moe_combine",
    )(dest, rgate, x1_tiles, ys3, ln_g3, ln_b3)


def _block_diag_gate_weights(wa, wx):
    n_l, n_dir, n_h, hd, _ = wa.shape
    hh = n_h // 2
    eye = jnp.eye(hh, dtype=wa.dtype)

    def bd(wm):
        wm = wm.reshape(n_l, n_dir, 2, hh, hd, hd)
        full = jnp.einsum("ldghij,hk->ldghikj", wm, eye)
        return full.reshape(n_l, n_dir, 2, hh * hd, hh * hd)

    return jnp.concatenate([bd(wa), bd(wx)], axis=-1).astype(BF16)


def kernel(x, w_in, hg_lb, hg_norm, lru_conv_w, lru_conv_b, lru_wa, lru_ba, lru_wx, lru_bx, lru_lambda, lru_norm,
           w_out, ln1_g, ln1_b, router_w, router_b, w_gate_up, b_gate_up, w_down, b_down, ln2_g, ln2_b):
    bsz, s, d = x.shape
    depth = w_in.shape[0]
    hg_w = hg_lb.shape[-1]
    lru_w = lru_lambda.shape[-1]
    n_exp = router_w.shape[-1]
    de = w_down.shape[2]
    t = bsz * s
    alpha = float((2 * depth) ** 0.25)
    block_rows = 2 * MXU_DIM
    assert w_in.shape[-1] == 5 * hg_w + 2 * lru_w and hg_w == lru_w
    assert s % 256 == 0 or s < 256

    p = jax.nn.softmax(hg_lb.astype(F32), axis=0)
    lower_bounds = jnp.clip(jnp.cumsum(p, axis=0) - p[0:1], 0.0, 1.0 - 1e-6)
    w_in_b = w_in.astype(BF16)
    w_out_b = w_out.astype(BF16)
    wg = _block_diag_gate_weights(lru_wa, lru_wx)
    gate_bias = jnp.stack([lru_ba, lru_bx], axis=2).astype(F32)
    row3 = lambda a: a.astype(F32).reshape(depth, 1, a.shape[-1])
    pad_experts = lambda a: jnp.pad(a.astype(F32), [(0, 0)] * (a.ndim - 1) + [(0, LANES - n_exp)])
    router_w_pad = pad_experts(router_w)
    router_b_pad = pad_experts(row3(router_b))
    b_gu4 = b_gate_up.astype(F32).reshape(depth, n_exp, 1, 2 * de)
    b_dn4 = b_down.astype(F32).reshape(depth, n_exp, 1, d)

    x2 = x.reshape(t, d)
    for layer in range(depth):
        proj = _in_proj(x2, w_in_b, layer)
        proj3 = proj.reshape(bsz, s, proj.shape[-1])
        o_f, o_b = _hgrn2(proj3, lower_bounds, layer, hg_w)
        h_f, h_b = _griffin(proj3, lru_conv_w, row3(lru_conv_b), wg, gate_bias, lru_lambda, layer, lru_w,
                            col_block=5)
        x1, ridx, rgate, counts_f32 = _mix(x2, o_f.reshape(t, hg_w), o_b.reshape(t, hg_w), h_f.reshape(t, lru_w),
                               h_b.reshape(t, lru_w), proj, w_out_b, row3(hg_norm), row3(lru_norm), row3(ln1_g),
                               row3(ln1_b), router_w_pad, router_b_pad, n_exp, layer, alpha, zg_block=4, zy_block=6)
        dest, counts, pstarts, blk_e, n_used, n_rows = _routing_tables(ridx, counts_f32, n_exp, block_rows)
        xs = _dispatch(x1, dest, counts, pstarts, n_rows, block_rows, d)
        ys = _moe_ffn(xs.reshape(-1, LANES), blk_e, n_used, w_gate_up, b_gu4, w_down, b_dn4, layer, block_rows)
        x2 = _combine(x1, ys.reshape(xs.shape), dest, rgate.T, row3(ln2_g), row3(ln2_b), layer, alpha)
    return x2.reshape(bsz, s, d)
```

```python
import functools

import jax
import jax.numpy as jnp
from jax import lax
from jax.experimental import pallas as pl
from jax.experimental.pallas import tpu as pltpu

F32 = jnp.float32
BF16 = jnp.bfloat16

HG_HEADS = 4
HG_CHUNK = 32
LB_FLOOR = 1e-30
LRU_HEADS = 8
LRU_C = 8.0
CONV_WIDTH = 4
TOP_K = 4
SWIGLU_LIMIT = 7.0
SWIGLU_ALPHA = 1.702
LN_EPS = 1e-5
RMS_EPS = 1e-6

LANES = 128
SUBLANES = 8
MXU_DIM = 256
VMEM_LIMIT_BYTES = 56 * 1024 * 1024

NT_DIMS = (((1,), (1,)), ((), ()))
TN_DIMS = (((0,), (0,)), ((), ()))


def _params(*semantics):
    return pltpu.CompilerParams(dimension_semantics=semantics, vmem_limit_bytes=VMEM_LIMIT_BYTES)


def _bdot(a, b):
    return jnp.dot(a.astype(BF16), b.astype(BF16), preferred_element_type=F32)


def _bdot_general(a, b, dims):
    return lax.dot_general(a.astype(BF16), b.astype(BF16), dims, preferred_element_type=F32)


def _layernorm(t, g, b):
    mu = jnp.mean(t, axis=-1, keepdims=True)
    c = t - mu
    var = jnp.mean(c * c, axis=-1, keepdims=True)
    return c * lax.rsqrt(var + LN_EPS) * g + b


def _load_row_tiles(ref, n_rows):
    per_row = ref.shape[0] // n_rows
    return jnp.concatenate([ref[pl.ds(j, n_rows, stride=per_row), :] for j in range(per_row)], axis=1)


def _store_row_tiles(ref, value):
    n_rows, d = value.shape
    per_row = d // LANES
    for j in range(per_row):
        ref[pl.ds(j, n_rows, stride=per_row), :] = value[:, j * LANES:(j + 1) * LANES]


def _in_proj_kernel(x_ref, w_ref, o_ref):
    o_ref[...] = jnp.dot(x_ref[...].astype(BF16), w_ref[...], preferred_element_type=F32)


def _in_proj(x2, w_in_bf16, layer):
    t, d = x2.shape
    n = w_in_bf16.shape[-1]
    tm = min(512, t)
    return pl.pallas_call(
        _in_proj_kernel,
        grid=(t // tm,),
        in_specs=[
            pl.BlockSpec((tm, d), lambda i: (i, 0)),
            pl.BlockSpec((None, d, n), lambda i: (layer, 0, 0)),
        ],
        out_specs=pl.BlockSpec((tm, n), lambda i: (i, 0)),
        out_shape=jax.ShapeDtypeStruct((t, n), F32),
        compiler_params=_params("parallel"),
        name="in_proj",
    )(x2, w_in_bf16)


def _cumsum_rows(x):
    n = x.shape[0]
    row = lax.broadcasted_iota(jnp.int32, x.shape, 0)
    s = 1
    while s < n:
        x = x + jnp.where(row >= s, pltpu.roll(x, s, 0), 0.0)
        s *= 2
    return x


def _hg_chunk(zq, zv, zf, lb, log_lb, log1m_lb, st_ref, o_ref, r0, reverse):
    c = HG_CHUNK
    q = zq * jax.nn.sigmoid(zq)
    e = jnp.exp(-jnp.abs(zf))
    log_sig = jnp.minimum(zf, 0.0) - jnp.log1p(e)
    sig_neg = jnp.where(zf >= 0.0, e, 1.0) / (1.0 + e)
    b = log1m_lb + log_sig
    logf = jnp.maximum(log_lb, b) + jnp.log1p(jnp.exp(-jnp.abs(log_lb - b)))
    k = (1.0 - lb) * sig_neg

    p = _cumsum_rows(logf)
    g_last = p[c - 1:c]
    if reverse:
        g = g_last - p + logf
        g_ref = g[c // 2:c // 2 + 1]
    else:
        g = p
        g_ref = g[c // 2 - 1:c // 2]
    qg = q * jnp.exp(g - g_ref)
    kg = k * jnp.exp(g_ref - g)
    kl = kg * jnp.exp(g_last - g_ref)
    qe = qg * jnp.exp(g_ref)
    dec = jnp.exp(g_last)

    row = lax.broadcasted_iota(jnp.int32, (c, c), 0)
    col = lax.broadcasted_iota(jnp.int32, (c, c), 1)
    keep = (col >= row) if reverse else (col <= row)
    hd = zq.shape[1] // HG_HEADS
    for h in range(HG_HEADS):
        sl = slice(h * hd, (h + 1) * hd)
        scores = jnp.where(keep, _bdot_general(qg[:, sl], kg[:, sl], NT_DIMS), 0.0)
        st = st_ref[h]
        o = _bdot(scores, zv[:, sl]) + _bdot_general(qe[:, sl], st, NT_DIMS)
        o_ref[pl.ds(r0, c), sl] = o
        st_ref[h] = st * dec[:, sl] + _bdot_general(zv[:, sl], kl[:, sl], TN_DIMS)


def _hgrn2_kernel(fqif_ref, bqi_ref, bzb_ref, lb_ref, of_ref, ob_ref, stf_ref, stb_ref):
    @pl.when(pl.program_id(1) == 0)
    def _():
        stf_ref[...] = jnp.zeros_like(stf_ref)
        stb_ref[...] = jnp.zeros_like(stb_ref)

    w = lb_ref.shape[1]
    lb = lb_ref[...]
    log_lb = jnp.log(jnp.maximum(lb, LB_FLOOR))
    log1m_lb = jnp.log1p(-lb)
    n_chunks = of_ref.shape[0] // HG_CHUNK

    def body(ci, carry):
        r0 = pl.multiple_of(ci * HG_CHUNK, HG_CHUNK)
        rows = pl.ds(r0, HG_CHUNK)
        _hg_chunk(fqif_ref[rows, 0:w], fqif_ref[rows, w:2 * w], fqif_ref[rows, 2 * w:3 * w],
                  lb[0:1], log_lb[0:1], log1m_lb[0:1], stf_ref, of_ref, r0, False)
        rb = pl.multiple_of((n_chunks - 1 - ci) * HG_CHUNK, HG_CHUNK)
        rows_b = pl.ds(rb, HG_CHUNK)
        _hg_chunk(bqi_ref[rows_b, 0:w], bqi_ref[rows_b, w:2 * w], bzb_ref[rows_b, :],
                  lb[1:2], log_lb[1:2], log1m_lb[1:2], stb_ref, ob_ref, rb, True)
        return carry

    lax.fori_loop(0, n_chunks, body, 0, unroll=4)


def _hgrn2(proj3, lower_bounds, layer, hg_width):
    bsz, s, _ = proj3.shape
    w = hg_width
    ts = min(512, s)
    ns = s // ts
    hd = w // HG_HEADS
    out = jax.ShapeDtypeStruct((bsz, s, w), F32)
    state = pltpu.VMEM((HG_HEADS, hd, hd), F32)
    return pl.pallas_call(
        _hgrn2_kernel,
        grid=(bsz, ns),
        in_specs=[
            pl.BlockSpec((None, ts, 3 * w), lambda b, j: (b, j, 0)),
            pl.BlockSpec((None, ts, 2 * w), lambda b, j: (b, ns - 1 - j, 0)),
            pl.BlockSpec((None, ts, w), lambda b, j: (b, ns - 1 - j, 3)),
            pl.BlockSpec((None, 2, w), lambda b, j: (layer, 0, 0)),
        ],
        out_specs=[
            pl.BlockSpec((None, ts, w), lambda b, j: (b, j, 0)),
            pl.BlockSpec((None, ts, w), lambda b, j: (b, ns - 1 - j, 0)),
        ],
        out_shape=[out, out],
        scratch_shapes=[state, state],
        compiler_params=_params("parallel", "arbitrary"),
        name="hgrn2",
    )(proj3, proj3, proj3, lower_bounds)


def _lin_scan(a, u, carry, h_ref, reverse):
    n, w = a.shape
    n_groups = n // SUBLANES
    a = a.reshape(n_groups, SUBLANES, w)
    u = u.reshape(n_groups, SUBLANES, w)
    sub = lax.broadcasted_iota(jnp.int32, a.shape, 1)
    s = 1
    while s < SUBLANES:
        shift = (SUBLANES - s) if reverse else s
        m = (sub < SUBLANES - s) if reverse else (sub >= s)
        a_sh = jnp.where(m, pltpu.roll(a, shift, 1), 1.0)
        u_sh = jnp.where(m, pltpu.roll(u, shift, 1), 0.0)
        u = u + a * u_sh
        a = a * a_sh
        s *= 2
    groups = range(n_groups)
    for g in (reversed(groups) if reverse else groups):
        hg = u[g] + a[g] * carry
        h_ref[g * SUBLANES:(g + 1) * SUBLANES, :] = hg
        carry = hg[0:1] if reverse else hg[SUBLANES - 1:SUBLANES]
    return carry


def _griffin_dir(main_ref, prev_ref, next_ref, is_first, is_last, cw, cb, wg_ref, bias, sp,
                 carry_ref, h_ref, reverse):
    ts, w = main_ref.shape
    ng = ts // SUBLANES
    groups = jnp.concatenate([jnp.where(is_first, 0.0, prev_ref[...]).reshape(1, SUBLANES, w),
                              main_ref[...].reshape(ng, SUBLANES, w),
                              jnp.where(is_last, 0.0, next_ref[...]).reshape(1, SUBLANES, w)], axis=0)
    sub = lax.broadcasted_iota(jnp.int32, (ng, SUBLANES, w), 1)
    down1 = pltpu.roll(groups, 1, 1)
    down2 = pltpu.roll(groups, 2, 1)
    up1 = pltpu.roll(groups, SUBLANES - 1, 1)
    x_m1 = jnp.where(sub >= 1, down1[1:ng + 1], down1[0:ng])
    x_m2 = jnp.where(sub >= 2, down2[1:ng + 1], down2[0:ng])
    x_p1 = jnp.where(sub < SUBLANES - 1, up1[1:ng + 1], up1[2:ng + 2])
    taps = (x_m2, x_m1, groups[1:ng + 1], x_p1)
    xc = cb.reshape(1, 1, w)
    for j in range(CONV_WIDTH):
        xc = xc + cw[j:j + 1].reshape(1, 1, w) * taps[j]
    xc = xc.reshape(ts, w)

    half = w // 2
    pre = [_bdot(xc[:, i * half:(i + 1) * half], wg_ref[i]) for i in range(2)]
    r_pre = jnp.concatenate([pre[0][:, :half], pre[1][:, :half]], axis=1) + bias[0:1]
    i_pre = jnp.concatenate([pre[0][:, half:], pre[1][:, half:]], axis=1) + bias[1:2]
    r = 0.5 + 0.5 * jnp.tanh(0.5 * r_pre)
    ig = 0.5 + 0.5 * jnp.tanh(0.5 * i_pre)
    log_a = (-LRU_C) * r * sp
    a = jnp.exp(log_a)
    u = jnp.sqrt(jnp.maximum(1.0 - a * a, 0.0)) * (ig * xc)
    carry_ref[0:1, :] = _lin_scan(a, u, carry_ref[0:1, :], h_ref, reverse)


def _griffin_kernel(fm_ref, fp_ref, fn_ref, bm_ref, bp_ref, bn_ref, cw_ref, cb_ref, wg_ref, bias_ref, lam_ref,
                    hf_ref, hb_ref, cf_ref, cbk_ref):
    j = pl.program_id(1)
    ns = pl.num_programs(1)

    @pl.when(j == 0)
    def _():
        cf_ref[...] = jnp.zeros_like(cf_ref)
        cbk_ref[...] = jnp.zeros_like(cbk_ref)

    cw = cw_ref[...]
    cb = cb_ref[...]
    sp = jax.nn.softplus(-lam_ref[...])
    _griffin_dir(fm_ref, fp_ref, fn_ref, j == 0, j == ns - 1, cw, cb, wg_ref.at[0], bias_ref[0], sp[0:1],
                 cf_ref, hf_ref, False)
    _griffin_dir(bm_ref, bp_ref, bn_ref, j == ns - 1, j == 0, cw, cb, wg_ref.at[1], bias_ref[1], sp[1:2],
                 cbk_ref, hb_ref, True)


def _griffin(proj3, conv_w, conv_b3, wg, gate_bias, lam, layer, lru_width, col_block):
    bsz, s, _ = proj3.shape
    w = lru_width
    ts = min(512, s)
    ns = s // ts
    tb = ts // SUBLANES
    nb8 = s // SUBLANES
    out = jax.ShapeDtypeStruct((bsz, s, w), F32)
    halo = (None, SUBLANES, w)
    return pl.pallas_call(
        _griffin_kernel,
        grid=(bsz, ns),
        in_specs=[
            pl.BlockSpec((None, ts, w), lambda b, j: (b, j, col_block)),
            pl.BlockSpec(halo, lambda b, j: (b, jnp.maximum(j * tb - 1, 0), col_block)),
            pl.BlockSpec(halo, lambda b, j: (b, jnp.minimum((j + 1) * tb, nb8 - 1), col_block)),
            pl.BlockSpec((None, ts, w), lambda b, j: (b, ns - 1 - j, col_block)),
            pl.BlockSpec(halo, lambda b, j: (b, jnp.maximum((ns - 1 - j) * tb - 1, 0), col_block)),
            pl.BlockSpec(halo, lambda b, j: (b, jnp.minimum((ns - j) * tb, nb8 - 1), col_block)),
            pl.BlockSpec((None, CONV_WIDTH, w), lambda b, j: (layer, 0, 0)),
            pl.BlockSpec((None, 1, w), lambda b, j: (layer, 0, 0)),
            pl.BlockSpec((None, 2, 2, w // 2, w), lambda b, j: (layer, 0, 0, 0, 0)),
            pl.BlockSpec((None, 2, 2, w), lambda b, j: (layer, 0, 0, 0)),
            pl.BlockSpec((None, 2, w), lambda b, j: (layer, 0, 0)),
        ],
        out_specs=[
            pl.BlockSpec((None, ts, w), lambda b, j: (b, j, 0)),
            pl.BlockSpec((None, ts, w), lambda b, j: (b, ns - 1 - j, 0)),
        ],
        out_shape=[out, out],
        scratch_shapes=[
            pltpu.VMEM((SUBLANES, w), F32),
            pltpu.VMEM((SUBLANES, w), F32),
        ],
        compiler_params=_params("parallel", "arbitrary"),
        name="griffin",
    )(proj3, proj3, proj3, proj3, proj3, proj3, conv_w, conv_b3, wg, gate_bias, lam)


def _gelu_tanh(x):
    return 0.5 * x * (1.0 + jnp.tanh(0.7978845608028654 * (x + 0.044715 * x * x * x)))


def _mix_kernel(x_ref, of_ref, ob_ref, zg_ref, hf_ref, hb_ref, zy_ref, wout_ref, hgn_ref, lrn_ref, g1_ref, b1_ref,
                rw_ref, rb_ref, tri_ref, x1_ref, ridx_ref, rgate_ref, counts_ref, cnt_ref, *, alpha, n_exp):
    @pl.when(pl.program_id(0) == 0)
    def _():
        cnt_ref[...] = jnp.zeros_like(cnt_ref)

    o = of_ref[...] + ob_ref[...]
    w = o.shape[1]
    hd = w // HG_HEADS
    parts = []
    for h in range(HG_HEADS):
        oh = o[:, h * hd:(h + 1) * hd]
        parts.append(oh * lax.rsqrt(jnp.mean(oh * oh, axis=-1, keepdims=True) + RMS_EPS))
    zg = zg_ref[...]
    o_hg = jnp.concatenate(parts, axis=1) * hgn_ref[...] * (zg * jax.nn.sigmoid(zg))
    hh = hf_ref[...] + hb_ref[...]
    o_lru = hh * lax.rsqrt(jnp.mean(hh * hh, axis=-1, keepdims=True) + RMS_EPS) * lrn_ref[...]
    o_lru = o_lru * _gelu_tanh(zy_ref[...])
    y = _bdot(o_hg, wout_ref[0:w, :]) + _bdot(o_lru, wout_ref[w:, :])
    x1 = _layernorm(alpha * x_ref[...] + y, g1_ref[...], b1_ref[...])
    _store_row_tiles(x1_ref, x1)

    rw = rw_ref[...]
    x_hi, w_hi = x1.astype(BF16), rw.astype(BF16)
    x_lo = (x1 - x_hi.astype(F32)).astype(BF16)
    w_lo = (rw - w_hi.astype(F32)).astype(BF16)
    n_pad = rw.shape[1]
    hi_both = jnp.dot(x_hi, jnp.concatenate([w_hi, w_lo], axis=1), preferred_element_type=F32)
    logits = (hi_both[:, :n_pad] + hi_both[:, n_pad:] + jnp.dot(x_lo, w_hi, preferred_element_type=F32)
              + rb_ref[...])

    lt = jnp.transpose(logits)[0:n_exp]
    eid = lax.broadcasted_iota(jnp.int32, lt.shape, 0)
    vals, idxs = [], []
    for _ in range(TOP_K):
        m = jnp.max(lt, axis=0, keepdims=True)
        idx = jnp.min(jnp.where(lt == m, eid, n_exp), axis=0, keepdims=True)
        vals.append(m)
        idxs.append(idx)
        lt = jnp.where(eid == idx, -jnp.inf, lt)
    exps = [jnp.exp(v - vals[0]) for v in vals]
    denom = exps[0] + exps[1] + exps[2] + exps[3]

    base = cnt_ref[:, 0:1]
    ranks = []
    for kk in range(TOP_K):
        onehot = (eid == idxs[kk]).astype(F32)
        before = jnp.dot(onehot.astype(BF16), tri_ref[...], preferred_element_type=F32) + base
        ranks.append(jnp.sum(onehot * before, axis=0, keepdims=True).astype(jnp.int32))
        base = base + jnp.sum(onehot, axis=1, keepdims=True)
    cnt_ref[...] = jnp.broadcast_to(base, cnt_ref.shape)
    counts_ref[...] = cnt_ref[...]

    out_row = lax.broadcasted_iota(jnp.int32, ridx_ref.shape, 0)
    ridx = jnp.zeros(ridx_ref.shape, jnp.int32)
    rgate = jnp.zeros(rgate_ref.shape, F32)
    for kk in range(TOP_K):
        ridx = jnp.where(out_row == kk, idxs[kk], ridx)
        ridx = jnp.where(out_row == TOP_K + kk, ranks[kk], ridx)
        rgate = jnp.where(out_row == kk, exps[kk] / denom, rgate)
    ridx_ref[...] = ridx
    rgate_ref[...] = rgate


def _mix(x2, o_f, o_b, h_f, h_b, proj, w_out_bf16, hg_norm3, lru_norm3, ln_g3, ln_b3, router_w_pad, router_b3_pad,
         n_exp, layer, alpha, zg_block, zy_block):
    t, d = x2.shape
    w = o_f.shape[1]
    assert router_w_pad.shape[-1] == LANES and n_exp <= LANES and n_exp % SUBLANES == 0
    tm = min(512, t)
    row_blk = lambda width: pl.BlockSpec((tm, width), lambda i: (i, 0))
    col_blk = pl.BlockSpec((2 * TOP_K, tm), lambda i: (0, i))
    vec = lambda width: pl.BlockSpec((None, 1, width), lambda i: (layer, 0, 0))
    return pl.pallas_call(
        functools.partial(_mix_kernel, alpha=alpha, n_exp=n_exp),
        grid=(t // tm,),
        in_specs=[
            row_blk(d), row_blk(w), row_blk(w),
            pl.BlockSpec((tm, w), lambda i: (i, zg_block)),
            row_blk(w), row_blk(w),
            pl.BlockSpec((tm, w), lambda i: (i, zy_block)),
            pl.BlockSpec((None, 2 * w, d), lambda i: (layer, 0, 0)),
            vec(w), vec(w), vec(d), vec(d),
            pl.BlockSpec((None, d, LANES), lambda i: (layer, 0, 0)),
            vec(LANES),
            pl.BlockSpec((tm, tm), lambda i: (0, 0)),
        ],
        out_specs=[pl.BlockSpec((tm * d // LANES, LANES), lambda i: (i, 0)), col_blk, col_blk,
                   pl.BlockSpec((n_exp, LANES), lambda i: (0, 0))],
        out_shape=[
            jax.ShapeDtypeStruct((t * d // LANES, LANES), F32),
            jax.ShapeDtypeStruct((2 * TOP_K, t), jnp.int32),
            jax.ShapeDtypeStruct((2 * TOP_K, t), F32),
            jax.ShapeDtypeStruct((n_exp, LANES), F32),
        ],
        scratch_shapes=[pltpu.VMEM((n_exp, LANES), F32)],
        compiler_params=_params("arbitrary"),
        name="mix_ln_router",
    )(x2, o_f, o_b, proj, h_f, h_b, proj, w_out_bf16, hg_norm3, lru_norm3, ln_g3, ln_b3, router_w_pad,
      router_b3_pad, jnp.triu(jnp.ones((tm, tm), BF16), k=1))


DMA_UNROLL_ROWS = 32


def _routing_tables(ridx, counts_f32, n_exp, block_rows):
    t = ridx.shape[1]
    idx = ridx[:TOP_K].T
    rank = ridx[TOP_K:2 * TOP_K].T
    counts = counts_f32[:, 0].astype(jnp.int32)
    padded = ((counts + block_rows - 1) // block_rows) * block_rows
    pends = jnp.cumsum(padded)
    pstarts = pends - padded
    onehot = idx[:, :, None] == jnp.arange(n_exp, dtype=jnp.int32)[None, None, :]
    dest = (jnp.sum(jnp.where(onehot, pstarts[None, None, :], 0), axis=-1) + rank).reshape(-1).astype(jnp.int32)
    n_rows = t * TOP_K + n_exp * block_rows
    n_blocks = n_rows // block_rows
    blk_start = jnp.arange(n_blocks, dtype=jnp.int32) * block_rows
    blk_e = jnp.minimum(jnp.searchsorted(pends, blk_start, side="right", method="compare_all"),
                        n_exp - 1).astype(jnp.int32)
    n_used = (pends[-1] // block_rows).astype(jnp.int32).reshape(1)
    return dest, counts, pstarts.astype(jnp.int32), blk_e, n_used, n_rows


def _dispatch_kernel(cnt_ref, pst_ref, dest_ref, x_ref, xs_ref, zero_ref, sem, *, block_rows):
    per_row = xs_ref.shape[1]
    tm = x_ref.shape[0] // per_row

    def issue(g, carry):
        r0 = pl.multiple_of(g * DMA_UNROLL_ROWS, DMA_UNROLL_ROWS)
        for j in range(DMA_UNROLL_ROWS):
            src = x_ref.at[pl.ds(pl.multiple_of((r0 + j) * per_row, per_row), per_row)]
            for kk in range(TOP_K):
                d = dest_ref[(r0 + j) * TOP_K + kk]
                pltpu.make_async_copy(src, xs_ref.at[d], sem).start(priority=kk % 2)
        return carry

    lax.fori_loop(0, tm // DMA_UNROLL_ROWS, issue, 0)
    for _ in range(TOP_K):
        pltpu.make_async_copy(xs_ref.at[pl.ds(0, tm)], xs_ref.at[pl.ds(0, tm)], sem).wait()

    @pl.when(pl.program_id(0) == pl.num_programs(0) - 1)
    def _():
        zero_ref[...] = jnp.zeros_like(zero_ref)
        n_exp = cnt_ref.shape[0]
        last = n_exp - 1
        cnt_last = cnt_ref[last]
        used_rows = pst_ref[last] + cnt_last + (block_rows - cnt_last % block_rows) % block_rows
        n_tail = (xs_ref.shape[0] - used_rows) // block_rows

        def tail_copy(b):
            start = pl.multiple_of(used_rows + b * block_rows, block_rows)
            return pltpu.make_async_copy(zero_ref, xs_ref.at[pl.ds(start, block_rows)], sem)

        def tail_start(b, c2):
            tail_copy(b).start()
            return c2

        def tail_wait(b, c2):
            tail_copy(b).wait()
            return c2

        lax.fori_loop(0, n_tail, tail_start, 0)
        lax.fori_loop(0, n_tail, tail_wait, 0)

        def per_expert(e, carry):
            cnt = cnt_ref[e]
            first = pst_ref[e] + cnt
            n_pad = (block_rows - cnt % block_rows) % block_rows

            def pieces(action):
                off = first
                for bit in range(block_rows.bit_length() - 1):
                    size = 1 << bit
                    copy = pltpu.make_async_copy(zero_ref.at[pl.ds(0, size)], xs_ref.at[pl.ds(off, size)], sem)
                    pl.when((n_pad & size) != 0)(functools.partial(action, copy))
                    off = off + (n_pad & size)

            pieces(lambda copy: copy.start())
            pieces(lambda copy: copy.wait())
            return carry

        lax.fori_loop(0, n_exp, per_expert, 0)


def _dispatch(x1_tiles, dest, counts, pstarts, n_rows, block_rows, d):
    per_row = d // LANES
    t = x1_tiles.shape[0] // per_row
    tm = min(512, t)
    assert tm % DMA_UNROLL_ROWS == 0 and block_rows & (block_rows - 1) == 0
    return pl.pallas_call(
        functools.partial(_dispatch_kernel, block_rows=block_rows),
        grid_spec=pltpu.PrefetchScalarGridSpec(
            num_scalar_prefetch=2,
            grid=(t // tm,),
            in_specs=[
                pl.BlockSpec((tm * TOP_K,), lambda i, c, p: (i,), memory_space=pltpu.SMEM),
                pl.BlockSpec((tm * per_row, LANES), lambda i, c, p: (i, 0)),
            ],
            out_specs=pl.BlockSpec(memory_space=pl.ANY),
            scratch_shapes=[pltpu.VMEM((block_rows, per_row, LANES), F32), pltpu.SemaphoreType.DMA],
        ),
        out_shape=jax.ShapeDtypeStruct((n_rows, per_row, LANES), F32),
        compiler_params=_params("arbitrary"),
        name="moe_dispatch",
    )(counts, pstarts, dest, x1_tiles)


def _moe_ffn_kernel(blk_e_ref, n_used_ref, xs_ref, wgu_f32_ref, bgu_ref, wdn_f32_ref, bdn_ref, ys_ref,
                    wgu_ref, wdn_ref):
    i = pl.program_id(0)
    used = i < n_used_ref[0]

    @pl.when(used & ((i == 0) | (blk_e_ref[i] != blk_e_ref[jnp.maximum(i - 1, 0)])))
    def _():
        rows = LANES

        def cast_gu(j, carry):
            sl = pl.ds(pl.multiple_of(j * rows, rows), rows)
            wgu_ref[sl, :] = wgu_f32_ref[sl, :].astype(BF16)
            return carry

        def cast_dn(j, carry):
            sl = pl.ds(pl.multiple_of(j * rows, rows), rows)
            wdn_ref[sl, :] = wdn_f32_ref[sl, :].astype(BF16)
            return carry

        lax.fori_loop(0, wgu_ref.shape[0] // rows, cast_gu, 0)
        lax.fori_loop(0, wdn_ref.shape[0] // rows, cast_dn, 0)

    @pl.when(used)
    def _():
        de = wdn_ref.shape[0]
        bm = xs_ref.shape[0] * LANES // wgu_ref.shape[0]
        x = _load_row_tiles(xs_ref, bm).astype(BF16)
        gu = jnp.dot(x, wgu_ref[...], preferred_element_type=F32) + bgu_ref[...]
        gate = jnp.minimum(gu[:, :de], SWIGLU_LIMIT)
        up = jnp.clip(gu[:, de:], -SWIGLU_LIMIT, SWIGLU_LIMIT)
        hid = (up + 1.0) * (gate * jax.nn.sigmoid(SWIGLU_ALPHA * gate))
        _store_row_tiles(ys_ref, jnp.dot(hid.astype(BF16), wdn_ref[...], preferred_element_type=F32) + bdn_ref[...])

    @pl.when(jnp.logical_not(used))
    def _():
        ys_ref[...] = jnp.zeros_like(ys_ref)


def _moe_ffn(xs_tiles, blk_e, n_used, w_gate_up, b_gu4, w_down, b_dn4, layer, block_rows):
    d, de = w_down.shape[3], w_down.shape[2]
    per_row = d // LANES
    n_blocks = xs_tiles.shape[0] // (block_rows * per_row)
    w_map = lambda i, be, nu: (layer, be[i], 0, 0)
    return pl.pallas_call(
        _moe_ffn_kernel,
        grid_spec=pltpu.PrefetchScalarGridSpec(
            num_scalar_prefetch=2,
            grid=(n_blocks,),
            in_specs=[
                pl.BlockSpec((block_rows * per_row, LANES), lambda i, be, nu: (jnp.minimum(i, nu[0] - 1), 0)),
                pl.BlockSpec((None, None, d, 2 * de), w_map),
                pl.BlockSpec((None, None, 1, 2 * de), w_map),
                pl.BlockSpec((None, None, de, d), w_map),
                pl.BlockSpec((None, None, 1, d), w_map),
            ],
            out_specs=pl.BlockSpec((block_rows * per_row, LANES), lambda i, be, nu: (i, 0)),
            scratch_shapes=[pltpu.VMEM((d, 2 * de), BF16), pltpu.VMEM((de, d), BF16)],
        ),
        out_shape=jax.ShapeDtypeStruct(xs_tiles.shape, F32),
        compiler_params=_params("arbitrary"),
        name="moe_ffn",
    )(blk_e, n_used, xs_tiles, w_gate_up, b_gu4, w_down, b_dn4)


def _combine_kernel(dest_ref, gate_ref, x1_ref, ys_ref, g2_ref, b2_ref, o_ref, buf0, buf1, sem, *, alpha):
    i = pl.program_id(0)
    n_tiles = pl.num_programs(0) - 1
    tm = o_ref.shape[0]
    per_row = ys_ref.shape[1]

    def issue(buf, slot):
        def body(g, carry):
            r0 = pl.multiple_of(g * DMA_UNROLL_ROWS, DMA_UNROLL_ROWS)
            for j in range(DMA_UNROLL_ROWS):
                rows = pl.ds(pl.multiple_of((r0 + j) * per_row, per_row), per_row)
                for kk in range(TOP_K):
                    src = dest_ref[(r0 + j) * TOP_K + kk]
                    pltpu.make_async_copy(ys_ref.at[src], buf.at[kk, rows], sem.at[slot]).start(priority=kk % 2)
            return carry

        lax.fori_loop(0, tm // DMA_UNROLL_ROWS, body, 0)

    def finish(buf, slot):
        for kk in range(TOP_K):
            pltpu.make_async_copy(ys_ref.at[pl.ds(0, tm)], ys_ref.at[pl.ds(0, tm)], sem.at[slot]).wait()
        gates = gate_ref[...]
        m = gates[:, 0:1] * _load_row_tiles(buf.at[0], tm)
        for kk in range(1, TOP_K):
            m = m + gates[:, kk:kk + 1] * _load_row_tiles(buf.at[kk], tm)
        o_ref[...] = _layernorm(alpha * _load_row_tiles(x1_ref, tm) + m, g2_ref[...], b2_ref[...])

    for parity, (cur, prev) in enumerate(((buf0, buf1), (buf1, buf0))):
        @pl.when(i % 2 == parity)
        def _():
            @pl.when(i < n_tiles)
            def _():
                issue(cur, parity)

            @pl.when(i > 0)
            def _():
                finish(prev, 1 - parity)


def _combine(x1_tiles, ys3, dest, rgate, ln_g3, ln_b3, layer, alpha):
    per_row = ys3.shape[1]
    d = per_row * LANES
    t = x1_tiles.shape[0] // per_row
    tm = min(256, t)
    nt = t // tm
    assert tm % DMA_UNROLL_ROWS == 0
    done = lambda i: (jnp.maximum(i - 1, 0), 0)
    row_buffer = pltpu.VMEM((TOP_K, tm * per_row, LANES), F32)
    return pl.pallas_call(
        functools.partial(_combine_kernel, alpha=alpha),
        grid=(nt + 1,),
        in_specs=[
            pl.BlockSpec((tm * TOP_K,), lambda i: (jnp.minimum(i, nt - 1),), memory_space=pltpu.SMEM),
            pl.BlockSpec((tm, rgate.shape[1]), done),
            pl.BlockSpec((tm * per_row, LANES), done),
            pl.BlockSpec(memory_space=pl.ANY),
            pl.BlockSpec((None, 1, d), lambda i: (layer, 0, 0)),
            pl.BlockSpec((None, 1, d), lambda i: (layer, 0, 0)),
        ],
        out_specs=pl.BlockSpec((tm, d), done),
        out_shape=jax.ShapeDtypeStruct((t, d), F32),
        scratch_shapes=[row_buffer, row_buffer, pltpu.SemaphoreType.DMA((2,))],
        compiler_params=_params("arbitrary"),
        name="moe_combine",
    )(dest, rgate, x1_tiles, ys3, ln_g3, ln_b3)


def _block_diag_gate_weights(wa, wx):
    n_l, n_dir, n_h, hd, _ = wa.shape
    hh = n_h // 2
    eye = jnp.eye(hh, dtype=wa.dtype)

    def bd(wm):
        wm = wm.reshape(n_l, n_dir, 2, hh, hd, hd)
        full = jnp.einsum("ldghij,hk->ldghikj", wm, eye)
        return full.reshape(n_l, n_dir, 2, hh * hd, hh * hd)

    return jnp.concatenate([bd(wa), bd(wx)], axis=-1).astype(BF16)


def kernel(x, w_in, hg_lb, hg_norm, lru_conv_w, lru_conv_b, lru_wa, lru_ba, lru_wx, lru_bx, lru_lambda, lru_norm,
           w_out, ln1_g, ln1_b, router_w, router_b, w_gate_up, b_gate_up, w_down, b_down, ln2_g, ln2_b):
    bsz, s, d = x.shape
    depth = w_in.shape[0]
    hg_w = hg_lb.shape[-1]
    lru_w = lru_lambda.shape[-1]
    n_exp = router_w.shape[-1]
    de = w_down.shape[2]
    t = bsz * s
    alpha = float((2 * depth) ** 0.25)
    block_rows = 2 * MXU_DIM
    assert w_in.shape[-1] == 5 * hg_w + 2 * lru_w and hg_w == lru_w
    assert s % 512 == 0 or s < 512 and s % HG_CHUNK == 0

    p = jax.nn.softmax(hg_lb.astype(F32), axis=0)
    lower_bounds = jnp.clip(jnp.cumsum(p, axis=0) - p[0:1], 0.0, 1.0 - 1e-6)
    w_in_b = w_in.astype(BF16)
    w_out_b = w_out.astype(BF16)
    wg = _block_diag_gate_weights(lru_wa, lru_wx)
    gate_bias = jnp.stack([lru_ba, lru_bx], axis=2).astype(F32)
    row3 = lambda a: a.astype(F32).reshape(depth, 1, a.shape[-1])
    pad_experts = lambda a: jnp.pad(a.astype(F32), [(0, 0)] * (a.ndim - 1) + [(0, LANES - n_exp)])
    router_w_pad = pad_experts(router_w)
    router_b_pad = pad_experts(row3(router_b))
    b_gu4 = b_gate_up.astype(F32).reshape(depth, n_exp, 1, 2 * de)
    b_dn4 = b_down.astype(F32).reshape(depth, n_exp, 1, d)

    x2 = x.reshape(t, d)
    for layer in range(depth):
        proj = _in_proj(x2, w_in_b, layer)
        proj3 = proj.reshape(bsz, s, proj.shape[-1])
        o_f, o_b = _hgrn2(proj3, lower_bounds, layer, hg_w)
        h_f, h_b = _griffin(proj3, lru_conv_w, row3(lru_conv_b), wg, gate_bias, lru_lambda, layer, lru_w,
                            col_block=5)
        x1, ridx, rgate, counts_f32 = _mix(x2, o_f.reshape(t, hg_w), o_b.reshape(t, hg_w), h_f.reshape(t, lru_w),
                               h_b.reshape(t, lru_w), proj, w_out_b, row3(hg_norm), row3(lru_norm), row3(ln1_g),
                               row3(ln1_b), router_w_pad, router_b_pad, n_exp, layer, alpha, zg_block=4, zy_block=6)
        dest, counts, pstarts, blk_e, n_used, n_rows = _routing_tables(ridx, counts_f32, n_exp, block_rows)
        xs = _dispatch(x1, dest, counts, pstarts, n_rows, block_rows, d)
        ys = _moe_ffn(xs.reshape(-1, LANES), blk_e, n_used, w_gate_up, b_gu4, w_down, b_dn4, layer, block_rows)
        x2 = _combine(x1, ys.reshape(xs.shape), dest, rgate.T, row3(ln2_g), row3(ln2_b), layer, alpha)
    return x2.reshape(bsz, s, d)
```

```python
import functools

import jax
import jax.numpy as jnp
from jax import lax
from jax.experimental import pallas as pl
from jax.experimental.pallas import tpu as pltpu

F32 = jnp.float32
BF16 = jnp.bfloat16

HG_HEADS = 4
HG_CHUNK = 32
LB_FLOOR = 1e-30
LRU_HEADS = 8
LRU_C = 8.0
CONV_WIDTH = 4
TOP_K = 4
SWIGLU_LIMIT = 7.0
SWIGLU_ALPHA = 1.702
LN_EPS = 1e-5
RMS_EPS = 1e-6

LANES = 128
SUBLANES = 8
MXU_DIM = 256
VMEM_LIMIT_BYTES = 56 * 1024 * 1024

NT_DIMS = (((1,), (1,)), ((), ()))
TN_DIMS = (((0,), (0,)), ((), ()))


def _params(*semantics):
    return pltpu.CompilerParams(dimension_semantics=semantics, vmem_limit_bytes=VMEM_LIMIT_BYTES)


def _bdot(a, b):
    return jnp.dot(a.astype(BF16), b.astype(BF16), preferred_element_type=F32)


def _bdot_general(a, b, dims):
    return lax.dot_general(a.astype(BF16), b.astype(BF16), dims, preferred_element_type=F32)


def _layernorm(t, g, b):
    mu = jnp.mean(t, axis=-1, keepdims=True)
    c = t - mu
    var = jnp.mean(c * c, axis=-1, keepdims=True)
    return c * lax.rsqrt(var + LN_EPS) * g + b


def _load_row_tiles(ref, n_rows):
    per_row = ref.shape[0] // n_rows
    return jnp.concatenate([ref[pl.ds(j, n_rows, stride=per_row), :] for j in range(per_row)], axis=1)


def _store_row_tiles(ref, value):
    n_rows, d = value.shape
    per_row = d // LANES
    for j in range(per_row):
        ref[pl.ds(j, n_rows, stride=per_row), :] = value[:, j * LANES:(j + 1) * LANES]


def _in_proj_kernel(x_ref, w_ref, o_ref):
    o_ref[...] = jnp.dot(x_ref[...].astype(BF16), w_ref[...], preferred_element_type=F32)


def _in_proj(x2, w_in_bf16, layer):
    t, d = x2.shape
    n = w_in_bf16.shape[-1]
    tm = min(512, t)
    return pl.pallas_call(
        _in_proj_kernel,
        grid=(t // tm,),
        in_specs=[
            pl.BlockSpec((tm, d), lambda i: (i, 0)),
            pl.BlockSpec((None, d, n), lambda i: (layer, 0, 0)),
        ],
        out_specs=pl.BlockSpec((tm, n), lambda i: (i, 0)),
        out_shape=jax.ShapeDtypeStruct((t, n), F32),
        compiler_params=_params("parallel"),
        name="in_proj",
    )(x2, w_in_bf16)


def _cumsum_rows(x):
    n = x.shape[0]
    row = lax.broadcasted_iota(jnp.int32, x.shape, 0)
    s = 1
    while s < n:
        x = x + jnp.where(row >= s, pltpu.roll(x, s, 0), 0.0)
        s *= 2
    return x


def _hg_chunk(zq, zv, zf, lb, log_lb, log1m_lb, st_ref, o_ref, r0, reverse):
    c = HG_CHUNK
    q = zq * jax.nn.sigmoid(zq)
    e = jnp.exp(-jnp.abs(zf))
    log_sig = jnp.minimum(zf, 0.0) - jnp.log1p(e)
    sig_neg = jnp.where(zf >= 0.0, e, 1.0) / (1.0 + e)
    b = log1m_lb + log_sig
    logf = jnp.maximum(log_lb, b) + jnp.log1p(jnp.exp(-jnp.abs(log_lb - b)))
    k = (1.0 - lb) * sig_neg

    p = _cumsum_rows(logf)
    g_last = p[c - 1:c]
    if reverse:
        g = g_last - p + logf
        g_ref = g[c // 2:c // 2 + 1]
    else:
        g = p
        g_ref = g[c // 2 - 1:c // 2]
    qg = q * jnp.exp(g - g_ref)
    kg = k * jnp.exp(g_ref - g)
    kl = kg * jnp.exp(g_last - g_ref)
    qe = qg * jnp.exp(g_ref)
    dec = jnp.exp(g_last)

    row = lax.broadcasted_iota(jnp.int32, (c, c), 0)
    col = lax.broadcasted_iota(jnp.int32, (c, c), 1)
    keep = (col >= row) if reverse else (col <= row)
    hd = zq.shape[1] // HG_HEADS
    for h in range(HG_HEADS):
        sl = slice(h * hd, (h + 1) * hd)
        scores = jnp.where(keep, _bdot_general(qg[:, sl], kg[:, sl], NT_DIMS), 0.0)
        st = st_ref[h]
        o = _bdot(scores, zv[:, sl]) + _bdot_general(qe[:, sl], st, NT_DIMS)
        o_ref[pl.ds(r0, c), sl] = o
        st_ref[h] = st * dec[:, sl] + _bdot_general(zv[:, sl], kl[:, sl], TN_DIMS)


def _hgrn2_kernel(fqif_ref, bqi_ref, bzb_ref, lb_ref, of_ref, ob_ref, stf_ref, stb_ref):
    @pl.when(pl.program_id(1) == 0)
    def _():
        stf_ref[...] = jnp.zeros_like(stf_ref)
        stb_ref[...] = jnp.zeros_like(stb_ref)

    w = lb_ref.shape[1]
    lb = lb_ref[...]
    log_lb = jnp.log(jnp.maximum(lb, LB_FLOOR))
    log1m_lb = jnp.log1p(-lb)
    n_chunks = of_ref.shape[0] // HG_CHUNK

    def body(ci, carry):
        r0 = pl.multiple_of(ci * HG_CHUNK, HG_CHUNK)
        rows = pl.ds(r0, HG_CHUNK)
        _hg_chunk(fqif_ref[rows, 0:w], fqif_ref[rows, w:2 * w], fqif_ref[rows, 2 * w:3 * w],
                  lb[0:1], log_lb[0:1], log1m_lb[0:1], stf_ref, of_ref, r0, False)
        rb = pl.multiple_of((n_chunks - 1 - ci) * HG_CHUNK, HG_CHUNK)
        rows_b = pl.ds(rb, HG_CHUNK)
        _hg_chunk(bqi_ref[rows_b, 0:w], bqi_ref[rows_b, w:2 * w], bzb_ref[rows_b, :],
                  lb[1:2], log_lb[1:2], log1m_lb[1:2], stb_ref, ob_ref, rb, True)
        return carry

    lax.fori_loop(0, n_chunks, body, 0, unroll=4)


def _hgrn2(proj3, lower_bounds, layer, hg_width):
    bsz, s, _ = proj3.shape
    w = hg_width
    ts = min(512, s)
    ns = s // ts
    hd = w // HG_HEADS
    out = jax.ShapeDtypeStruct((bsz, s, w), F32)
    state = pltpu.VMEM((HG_HEADS, hd, hd), F32)
    return pl.pallas_call(
        _hgrn2_kernel,
        grid=(bsz, ns),
        in_specs=[
            pl.BlockSpec((None, ts, 3 * w), lambda b, j: (b, j, 0)),
            pl.BlockSpec((None, ts, 2 * w), lambda b, j: (b, ns - 1 - j, 0)),
            pl.BlockSpec((None, ts, w), lambda b, j: (b, ns - 1 - j, 3)),
            pl.BlockSpec((None, 2, w), lambda b, j: (layer, 0, 0)),
        ],
        out_specs=[
            pl.BlockSpec((None, ts, w), lambda b, j: (b, j, 0)),
            pl.BlockSpec((None, ts, w), lambda b, j: (b, ns - 1 - j, 0)),
        ],
        out_shape=[out, out],
        scratch_shapes=[state, state],
        compiler_params=_params("parallel", "arbitrary"),
        name="hgrn2",
    )(proj3, proj3, proj3, lower_bounds)


def _lin_scan(a, u, carry, h_ref, reverse):
    n, w = a.shape
    n_groups = n // SUBLANES
    a = a.reshape(n_groups, SUBLANES, w)
    u = u.reshape(n_groups, SUBLANES, w)
    sub = lax.broadcasted_iota(jnp.int32, a.shape, 1)
    s = 1
    while s < SUBLANES:
        shift = (SUBLANES - s) if reverse else s
        m = (sub < SUBLANES - s) if reverse else (sub >= s)
        a_sh = jnp.where(m, pltpu.roll(a, shift, 1), 1.0)
        u_sh = jnp.where(m, pltpu.roll(u, shift, 1), 0.0)
        u = u + a * u_sh
        a = a * a_sh
        s *= 2
    groups = range(n_groups)
    for g in (reversed(groups) if reverse else groups):
        hg = u[g] + a[g] * carry
        h_ref[g * SUBLANES:(g + 1) * SUBLANES, :] = hg
        carry = hg[0:1] if reverse else hg[SUBLANES - 1:SUBLANES]
    return carry


def _griffin_dir(main_ref, prev_ref, next_ref, is_first, is_last, cw, cb, wg_ref, bias, sp,
                 carry_ref, h_ref, reverse):
    ts, w = main_ref.shape
    ng = ts // SUBLANES
    groups = jnp.concatenate([jnp.where(is_first, 0.0, prev_ref[...]).reshape(1, SUBLANES, w),
                              main_ref[...].reshape(ng, SUBLANES, w),
                              jnp.where(is_last, 0.0, next_ref[...]).reshape(1, SUBLANES, w)], axis=0)
    sub = lax.broadcasted_iota(jnp.int32, (ng, SUBLANES, w), 1)
    down1 = pltpu.roll(groups, 1, 1)
    down2 = pltpu.roll(groups, 2, 1)
    up1 = pltpu.roll(groups, SUBLANES - 1, 1)
    x_m1 = jnp.where(sub >= 1, down1[1:ng + 1], down1[0:ng])
    x_m2 = jnp.where(sub >= 2, down2[1:ng + 1], down2[0:ng])
    x_p1 = jnp.where(sub < SUBLANES - 1, up1[1:ng + 1], up1[2:ng + 2])
    taps = (x_m2, x_m1, groups[1:ng + 1], x_p1)
    xc = cb.reshape(1, 1, w)
    for j in range(CONV_WIDTH):
        xc = xc + cw[j:j + 1].reshape(1, 1, w) * taps[j]
    xc = xc.reshape(ts, w)

    half = w // 2
    pre = [_bdot(xc[:, i * half:(i + 1) * half], wg_ref[i]) for i in range(2)]
    r_pre = jnp.concatenate([pre[0][:, :half], pre[1][:, :half]], axis=1) + bias[0:1]
    i_pre = jnp.concatenate([pre[0][:, half:], pre[1][:, half:]], axis=1) + bias[1:2]
    r = 0.5 + 0.5 * jnp.tanh(0.5 * r_pre)
    ig = 0.5 + 0.5 * jnp.tanh(0.5 * i_pre)
    log_a = (-LRU_C) * r * sp
    a = jnp.exp(log_a)
    u = jnp.sqrt(jnp.maximum(1.0 - a * a, 0.0)) * (ig * xc)
    carry_ref[0:1, :] = _lin_scan(a, u, carry_ref[0:1, :], h_ref, reverse)


def _griffin_kernel(fm_ref, fp_ref, fn_ref, bm_ref, bp_ref, bn_ref, cw_ref, cb_ref, wg_ref, bias_ref, lam_ref,
                    hf_ref, hb_ref, cf_ref, cbk_ref):
    j = pl.program_id(1)
    ns = pl.num_programs(1)

    @pl.when(j == 0)
    def _():
        cf_ref[...] = jnp.zeros_like(cf_ref)
        cbk_ref[...] = jnp.zeros_like(cbk_ref)

    cw = cw_ref[...]
    cb = cb_ref[...]
    sp = jax.nn.softplus(-lam_ref[...])
    _griffin_dir(fm_ref, fp_ref, fn_ref, j == 0, j == ns - 1, cw, cb, wg_ref.at[0], bias_ref[0], sp[0:1],
                 cf_ref, hf_ref, False)
    _griffin_dir(bm_ref, bp_ref, bn_ref, j == ns - 1, j == 0, cw, cb, wg_ref.at[1], bias_ref[1], sp[1:2],
                 cbk_ref, hb_ref, True)


def _griffin(proj3, conv_w, conv_b3, wg, gate_bias, lam, layer, lru_width, col_block):
    bsz, s, _ = proj3.shape
    w = lru_width
    ts = min(512, s)
    ns = s // ts
    tb = ts // SUBLANES
    nb8 = s // SUBLANES
    out = jax.ShapeDtypeStruct((bsz, s, w), F32)
    halo = (None, SUBLANES, w)
    return pl.pallas_call(
        _griffin_kernel,
        grid=(bsz, ns),
        in_specs=[
            pl.BlockSpec((None, ts, w), lambda b, j: (b, j, col_block)),
            pl.BlockSpec(halo, lambda b, j: (b, jnp.maximum(j * tb - 1, 0), col_block)),
            pl.BlockSpec(halo, lambda b, j: (b, jnp.minimum((j + 1) * tb, nb8 - 1), col_block)),
            pl.BlockSpec((None, ts, w), lambda b, j: (b, ns - 1 - j, col_block)),
            pl.BlockSpec(halo, lambda b, j: (b, jnp.maximum((ns - 1 - j) * tb - 1, 0), col_block)),
            pl.BlockSpec(halo, lambda b, j: (b, jnp.minimum((ns - j) * tb, nb8 - 1), col_block)),
            pl.BlockSpec((None, CONV_WIDTH, w), lambda b, j: (layer, 0, 0)),
            pl.BlockSpec((None, 1, w), lambda b, j: (layer, 0, 0)),
            pl.BlockSpec((None, 2, 2, w // 2, w), lambda b, j: (layer, 0, 0, 0, 0)),
            pl.BlockSpec((None, 2, 2, w), lambda b, j: (layer, 0, 0, 0)),
            pl.BlockSpec((None, 2, w), lambda b, j: (layer, 0, 0)),
        ],
        out_specs=[
            pl.BlockSpec((None, ts, w), lambda b, j: (b, j, 0)),
            pl.BlockSpec((None, ts, w), lambda b, j: (b, ns - 1 - j, 0)),
        ],
        out_shape=[out, out],
        scratch_shapes=[
            pltpu.VMEM((SUBLANES, w), F32),
            pltpu.VMEM((SUBLANES, w), F32),
        ],
        compiler_params=_params("parallel", "arbitrary"),
        name="griffin",
    )(proj3, proj3, proj3, proj3, proj3, proj3, conv_w, conv_b3, wg, gate_bias, lam)


def _gelu_tanh(x):
    return 0.5 * x * (1.0 + jnp.tanh(0.7978845608028654 * (x + 0.044715 * x * x * x)))


def _mix_kernel(x_ref, of_ref, ob_ref, zg_ref, hf_ref, hb_ref, zy_ref, wout_ref, hgn_ref, lrn_ref, g1_ref, b1_ref,
                rw_ref, rb_ref, tri_ref, x1_ref, ridx_ref, rgate_ref, counts_ref, cnt_ref, *, alpha, n_exp):
    @pl.when(pl.program_id(0) == 0)
    def _():
        cnt_ref[...] = jnp.zeros_like(cnt_ref)

    o = of_ref[...] + ob_ref[...]
    w = o.shape[1]
    hd = w // HG_HEADS
    parts = []
    for h in range(HG_HEADS):
        oh = o[:, h * hd:(h + 1) * hd]
        parts.append(oh * lax.rsqrt(jnp.mean(oh * oh, axis=-1, keepdims=True) + RMS_EPS))
    zg = zg_ref[...]
    o_hg = jnp.concatenate(parts, axis=1) * hgn_ref[...] * (zg * jax.nn.sigmoid(zg))
    hh = hf_ref[...] + hb_ref[...]
    o_lru = hh * lax.rsqrt(jnp.mean(hh * hh, axis=-1, keepdims=True) + RMS_EPS) * lrn_ref[...]
    o_lru = o_lru * _gelu_tanh(zy_ref[...])
    y = _bdot(o_hg, wout_ref[0:w, :]) + _bdot(o_lru, wout_ref[w:, :])
    x1 = _layernorm(alpha * x_ref[...] + y, g1_ref[...], b1_ref[...])
    _store_row_tiles(x1_ref, x1)

    rw = rw_ref[...]
    x_hi, w_hi = x1.astype(BF16), rw.astype(BF16)
    x_lo = (x1 - x_hi.astype(F32)).astype(BF16)
    w_lo = (rw - w_hi.astype(F32)).astype(BF16)
    n_pad = rw.shape[1]
    hi_both = jnp.dot(x_hi, jnp.concatenate([w_hi, w_lo], axis=1), preferred_element_type=F32)
    logits = (hi_both[:, :n_pad] + hi_both[:, n_pad:] + jnp.dot(x_lo, w_hi, preferred_element_type=F32)
              + rb_ref[...])

    lt = jnp.transpose(logits)[0:n_exp]
    eid = lax.broadcasted_iota(jnp.int32, lt.shape, 0)
    vals, idxs = [], []
    for _ in range(TOP_K):
        m = jnp.max(lt, axis=0, keepdims=True)
        idx = jnp.min(jnp.where(lt == m, eid, n_exp), axis=0, keepdims=True)
        vals.append(m)
        idxs.append(idx)
        lt = jnp.where(eid == idx, -jnp.inf, lt)
    exps = [jnp.exp(v - vals[0]) for v in vals]
    denom = exps[0] + exps[1] + exps[2] + exps[3]

    base = cnt_ref[:, 0:1]
    ranks = []
    for kk in range(TOP_K):
        onehot = (eid == idxs[kk]).astype(F32)
        before = jnp.dot(onehot.astype(BF16), tri_ref[...], preferred_element_type=F32) + base
        ranks.append(jnp.sum(onehot * before, axis=0, keepdims=True).astype(jnp.int32))
        base = base + jnp.sum(onehot, axis=1, keepdims=True)
    cnt_ref[...] = jnp.broadcast_to(base, cnt_ref.shape)
    counts_ref[...] = cnt_ref[...]

    out_row = lax.broadcasted_iota(jnp.int32, ridx_ref.shape, 0)
    ridx = jnp.zeros(ridx_ref.shape, jnp.int32)
    rgate = jnp.zeros(rgate_ref.shape, F32)
    for kk in range(TOP_K):
        ridx = jnp.where(out_row == kk, idxs[kk], ridx)
        ridx = jnp.where(out_row == TOP_K + kk, ranks[kk], ridx)
        rgate = jnp.where(out_row == kk, exps[kk] / denom, rgate)
    ridx_ref[...] = ridx
    rgate_ref[...] = rgate


def _mix(x2, o_f, o_b, h_f, h_b, proj, w_out_bf16, hg_norm3, lru_norm3, ln_g3, ln_b3, router_w_pad, router_b3_pad,
         n_exp, layer, alpha, zg_block, zy_block):
    t, d = x2.shape
    w = o_f.shape[1]
    assert router_w_pad.shape[-1] == LANES and n_exp <= LANES and n_exp % SUBLANES == 0
    tm = min(512, t)
    row_blk = lambda width: pl.BlockSpec((tm, width), lambda i: (i, 0))
    col_blk = pl.BlockSpec((2 * TOP_K, tm), lambda i: (0, i))
    vec = lambda width: pl.BlockSpec((None, 1, width), lambda i: (layer, 0, 0))
    return pl.pallas_call(
        functools.partial(_mix_kernel, alpha=alpha, n_exp=n_exp),
        grid=(t // tm,),
        in_specs=[
            row_blk(d), row_blk(w), row_blk(w),
            pl.BlockSpec((tm, w), lambda i: (i, zg_block)),
            row_blk(w), row_blk(w),
            pl.BlockSpec((tm, w), lambda i: (i, zy_block)),
            pl.BlockSpec((None, 2 * w, d), lambda i: (layer, 0, 0)),
            vec(w), vec(w), vec(d), vec(d),
            pl.BlockSpec((None, d, LANES), lambda i: (layer, 0, 0)),
            vec(LANES),
            pl.BlockSpec((tm, tm), lambda i: (0, 0)),
        ],
        out_specs=[pl.BlockSpec((tm * d // LANES, LANES), lambda i: (i, 0)), col_blk, col_blk,
                   pl.BlockSpec((n_exp, LANES), lambda i: (0, 0))],
        out_shape=[
            jax.ShapeDtypeStruct((t * d // LANES, LANES), F32),
            jax.ShapeDtypeStruct((2 * TOP_K, t), jnp.int32),
            jax.ShapeDtypeStruct((2 * TOP_K, t), F32),
            jax.ShapeDtypeStruct((n_exp, LANES), F32),
        ],
        scratch_shapes=[pltpu.VMEM((n_exp, LANES), F32)],
        compiler_params=_params("arbitrary"),
        name="mix_ln_router",
    )(x2, o_f, o_b, proj, h_f, h_b, proj, w_out_bf16, hg_norm3, lru_norm3, ln_g3, ln_b3, router_w_pad,
      router_b3_pad, jnp.triu(jnp.ones((tm, tm), BF16), k=1))


DMA_UNROLL_ROWS = 32


def _routing_tables(ridx, counts_f32, n_exp, block_rows):
    t = ridx.shape[1]
    idx = ridx[:TOP_K].T
    rank = ridx[TOP_K:2 * TOP_K].T
    counts = counts_f32[:, 0].astype(jnp.int32)
    padded = ((counts + block_rows - 1) // block_rows) * block_rows
    pends = jnp.cumsum(padded)
    pstarts = pends - padded
    onehot = idx[:, :, None] == jnp.arange(n_exp, dtype=jnp.int32)[None, None, :]
    dest = (jnp.sum(jnp.where(onehot, pstarts[None, None, :], 0), axis=-1) + rank).reshape(-1).astype(jnp.int32)
    n_rows = t * TOP_K + n_exp * block_rows
    n_blocks = n_rows // block_rows
    blk_start = jnp.arange(n_blocks, dtype=jnp.int32) * block_rows
    blk_e = jnp.minimum(jnp.searchsorted(pends, blk_start, side="right", method="compare_all"),
                        n_exp - 1).astype(jnp.int32)
    n_used = (pends[-1] // block_rows).astype(jnp.int32).reshape(1)
    blk_rows = jnp.clip(counts[blk_e] - (blk_start - pstarts[blk_e]), 0, block_rows).astype(jnp.int32)
    return dest, counts, pstarts.astype(jnp.int32), blk_e, n_used, blk_rows, n_rows


def _dispatch_kernel(cnt_ref, pst_ref, dest_ref, x_ref, xs_ref, zero_ref, sem, *, block_rows):
    per_row = xs_ref.shape[1]
    tm = x_ref.shape[0] // per_row

    def issue(g, carry):
        r0 = pl.multiple_of(g * DMA_UNROLL_ROWS, DMA_UNROLL_ROWS)
        for j in range(DMA_UNROLL_ROWS):
            src = x_ref.at[pl.ds(pl.multiple_of((r0 + j) * per_row, per_row), per_row)]
            for kk in range(TOP_K):
                d = dest_ref[(r0 + j) * TOP_K + kk]
                pltpu.make_async_copy(src, xs_ref.at[d], sem).start(priority=kk % 2)
        return carry

    lax.fori_loop(0, tm // DMA_UNROLL_ROWS, issue, 0)
    for _ in range(TOP_K):
        pltpu.make_async_copy(xs_ref.at[pl.ds(0, tm)], xs_ref.at[pl.ds(0, tm)], sem).wait()

    @pl.when(pl.program_id(0) == pl.num_programs(0) - 1)
    def _():
        zero_ref[...] = jnp.zeros_like(zero_ref)
        n_exp = cnt_ref.shape[0]
        last = n_exp - 1
        cnt_last = cnt_ref[last]
        used_rows = pst_ref[last] + cnt_last + (block_rows - cnt_last % block_rows) % block_rows
        n_tail = (xs_ref.shape[0] - used_rows) // block_rows

        def tail_copy(b):
            start = pl.multiple_of(used_rows + b * block_rows, block_rows)
            return pltpu.make_async_copy(zero_ref, xs_ref.at[pl.ds(start, block_rows)], sem)

        def tail_start(b, c2):
            tail_copy(b).start()
            return c2

        def tail_wait(b, c2):
            tail_copy(b).wait()
            return c2

        lax.fori_loop(0, n_tail, tail_start, 0)
        lax.fori_loop(0, n_tail, tail_wait, 0)

        def per_expert(e, carry):
            cnt = cnt_ref[e]
            first = pst_ref[e] + cnt
            n_pad = (block_rows - cnt % block_rows) % block_rows

            def pieces(action):
                off = first
                for bit in range(block_rows.bit_length() - 1):
                    size = 1 << bit
                    copy = pltpu.make_async_copy(zero_ref.at[pl.ds(0, size)], xs_ref.at[pl.ds(off, size)], sem)
                    pl.when((n_pad & size) != 0)(functools.partial(action, copy))
                    off = off + (n_pad & size)

            pieces(lambda copy: copy.start())
            pieces(lambda copy: copy.wait())
            return carry

        lax.fori_loop(0, n_exp, per_expert, 0)


def _dispatch(x1_tiles, dest, counts, pstarts, n_rows, block_rows, d):
    per_row = d // LANES
    t = x1_tiles.shape[0] // per_row
    tm = min(512, t)
    assert tm % DMA_UNROLL_ROWS == 0 and block_rows & (block_rows - 1) == 0
    return pl.pallas_call(
        functools.partial(_dispatch_kernel, block_rows=block_rows),
        grid_spec=pltpu.PrefetchScalarGridSpec(
            num_scalar_prefetch=2,
            grid=(t // tm,),
            in_specs=[
                pl.BlockSpec((tm * TOP_K,), lambda i, c, p: (i,), memory_space=pltpu.SMEM),
                pl.BlockSpec((tm * per_row, LANES), lambda i, c, p: (i, 0)),
            ],
            out_specs=pl.BlockSpec(memory_space=pl.ANY),
            scratch_shapes=[pltpu.VMEM((block_rows, per_row, LANES), F32), pltpu.SemaphoreType.DMA],
        ),
        out_shape=jax.ShapeDtypeStruct((n_rows, per_row, LANES), F32),
        compiler_params=_params("arbitrary"),
        name="moe_dispatch",
    )(counts, pstarts, dest, x1_tiles)


MOE_ROW_SPLITS = 4


def _moe_ffn_kernel(blk_e_ref, n_used_ref, blk_rows_ref, xs_ref, wgu_f32_ref, bgu_ref, wdn_f32_ref, bdn_ref, ys_ref,
                    wgu_ref, wdn_ref):
    i = pl.program_id(0)
    valid = blk_rows_ref[i]
    used = valid > 0

    @pl.when(used & ((i == 0) | (blk_e_ref[i] != blk_e_ref[jnp.maximum(i - 1, 0)])))
    def _():
        rows = LANES

        def cast_gu(j, carry):
            sl = pl.ds(pl.multiple_of(j * rows, rows), rows)
            wgu_ref[sl, :] = wgu_f32_ref[sl, :].astype(BF16)
            return carry

        def cast_dn(j, carry):
            sl = pl.ds(pl.multiple_of(j * rows, rows), rows)
            wdn_ref[sl, :] = wdn_f32_ref[sl, :].astype(BF16)
            return carry

        lax.fori_loop(0, wgu_ref.shape[0] // rows, cast_gu, 0)
        lax.fori_loop(0, wdn_ref.shape[0] // rows, cast_dn, 0)

    de = wdn_ref.shape[0]
    per_row = wgu_ref.shape[0] // LANES
    bm = xs_ref.shape[0] // per_row
    quarter = bm // MOE_ROW_SPLITS
    for n in range(1, MOE_ROW_SPLITS + 1):
        m = n * quarter

        @pl.when((valid > m - quarter) & (valid <= m))
        def _():
            x = _load_row_tiles(xs_ref.at[pl.ds(0, m * per_row)], m).astype(BF16)
            gu = jnp.dot(x, wgu_ref[...], preferred_element_type=F32) + bgu_ref[...]
            gate = jnp.minimum(gu[:, :de], SWIGLU_LIMIT)
            up = jnp.clip(gu[:, de:], -SWIGLU_LIMIT, SWIGLU_LIMIT)
            hid = (up + 1.0) * (gate * jax.nn.sigmoid(SWIGLU_ALPHA * gate))
            y = jnp.dot(hid.astype(BF16), wdn_ref[...], preferred_element_type=F32) + bdn_ref[...]
            _store_row_tiles(ys_ref.at[pl.ds(0, m * per_row)], y)
            if m < bm:
                ys_ref[pl.ds(m * per_row, (bm - m) * per_row), :] = jnp.zeros(((bm - m) * per_row, LANES), F32)

    @pl.when(jnp.logical_not(used))
    def _():
        ys_ref[...] = jnp.zeros_like(ys_ref)


def _moe_ffn(xs_tiles, blk_e, n_used, blk_rows, w_gate_up, b_gu4, w_down, b_dn4, layer, block_rows):
    d, de = w_down.shape[3], w_down.shape[2]
    per_row = d // LANES
    n_blocks = xs_tiles.shape[0] // (block_rows * per_row)
    assert block_rows % (MOE_ROW_SPLITS * SUBLANES) == 0
    w_map = lambda i, be, nu, br: (layer, be[i], 0, 0)
    return pl.pallas_call(
        _moe_ffn_kernel,
        grid_spec=pltpu.PrefetchScalarGridSpec(
            num_scalar_prefetch=3,
            grid=(n_blocks,),
            in_specs=[
                pl.BlockSpec((block_rows * per_row, LANES), lambda i, be, nu, br: (jnp.minimum(i, nu[0] - 1), 0)),
                pl.BlockSpec((None, None, d, 2 * de), w_map),
                pl.BlockSpec((None, None, 1, 2 * de), w_map),
                pl.BlockSpec((None, None, de, d), w_map),
                pl.BlockSpec((None, None, 1, d), w_map),
            ],
            out_specs=pl.BlockSpec((block_rows * per_row, LANES), lambda i, be, nu, br: (i, 0)),
            scratch_shapes=[pltpu.VMEM((d, 2 * de), BF16), pltpu.VMEM((de, d), BF16)],
        ),
        out_shape=jax.ShapeDtypeStruct(xs_tiles.shape, F32),
        compiler_params=_params("arbitrary"),
        name="moe_ffn",
    )(blk_e, n_used, blk_rows, xs_tiles, w_gate_up, b_gu4, w_down, b_dn4)


def _combine_kernel(dest_ref, gate_ref, x1_ref, ys_ref, g2_ref, b2_ref, o_ref, buf0, buf1, sem, *, alpha):
    i = pl.program_id(0)
    n_tiles = pl.num_programs(0) - 1
    tm = o_ref.shape[0]
    per_row = ys_ref.shape[1]

    def issue(buf, slot):
        def body(g, carry):
            r0 = pl.multiple_of(g * DMA_UNROLL_ROWS, DMA_UNROLL_ROWS)
            for j in range(DMA_UNROLL_ROWS):
                rows = pl.ds(pl.multiple_of((r0 + j) * per_row, per_row), per_row)
                for kk in range(TOP_K):
                    src = dest_ref[(r0 + j) * TOP_K + kk]
                    pltpu.make_async_copy(ys_ref.at[src], buf.at[kk, rows], sem.at[slot]).start(priority=kk % 2)
            return carry

        lax.fori_loop(0, tm // DMA_UNROLL_ROWS, body, 0)

    def finish(buf, slot):
        for kk in range(TOP_K):
            pltpu.make_async_copy(ys_ref.at[pl.ds(0, tm)], ys_ref.at[pl.ds(0, tm)], sem.at[slot]).wait()
        gates = gate_ref[...]
        m = gates[:, 0:1] * _load_row_tiles(buf.at[0], tm)
        for kk in range(1, TOP_K):
            m = m + gates[:, kk:kk + 1] * _load_row_tiles(buf.at[kk], tm)
        o_ref[...] = _layernorm(alpha * _load_row_tiles(x1_ref, tm) + m, g2_ref[...], b2_ref[...])

    for parity, (cur, prev) in enumerate(((buf0, buf1), (buf1, buf0))):
        @pl.when(i % 2 == parity)
        def _():
            @pl.when(i < n_tiles)
            def _():
                issue(cur, parity)

            @pl.when(i > 0)
            def _():
                finish(prev, 1 - parity)


def _combine(x1_tiles, ys3, dest, rgate, ln_g3, ln_b3, layer, alpha):
    per_row = ys3.shape[1]
    d = per_row * LANES
    t = x1_tiles.shape[0] // per_row
    tm = min(256, t)
    nt = t // tm
    assert tm % DMA_UNROLL_ROWS == 0
    done = lambda i: (jnp.maximum(i - 1, 0), 0)
    row_buffer = pltpu.VMEM((TOP_K, tm * per_row, LANES), F32)
    return pl.pallas_call(
        functools.partial(_combine_kernel, alpha=alpha),
        grid=(nt + 1,),
        in_specs=[
            pl.BlockSpec((tm * TOP_K,), lambda i: (jnp.minimum(i, nt - 1),), memory_space=pltpu.SMEM),
            pl.BlockSpec((tm, rgate.shape[1]), done),
            pl.BlockSpec((tm * per_row, LANES), done),
            pl.BlockSpec(memory_space=pl.ANY),
            pl.BlockSpec((None, 1, d), lambda i: (layer, 0, 0)),
            pl.BlockSpec((None, 1, d), lambda i: (layer, 0, 0)),
        ],
        out_specs=pl.BlockSpec((tm, d), done),
        out_shape=jax.ShapeDtypeStruct((t, d), F32),
        scratch_shapes=[row_buffer, row_buffer, pltpu.SemaphoreType.DMA((2,))],
        compiler_params=_params("arbitrary"),
        name="moe_combine",
    )(dest, rgate, x1_tiles, ys3, ln_g3, ln_b3)


def _block_diag_gate_weights(wa, wx):
    n_l, n_dir, n_h, hd, _ = wa.shape
    hh = n_h // 2
    eye = jnp.eye(hh, dtype=wa.dtype)

    def bd(wm):
        wm = wm.reshape(n_l, n_dir, 2, hh, hd, hd)
        full = jnp.einsum("ldghij,hk->ldghikj", wm, eye)
        return full.reshape(n_l, n_dir, 2, hh * hd, hh * hd)

    return jnp.concatenate([bd(wa), bd(wx)], axis=-1).astype(BF16)


def kernel(x, w_in, hg_lb, hg_norm, lru_conv_w, lru_conv_b, lru_wa, lru_ba, lru_wx, lru_bx, lru_lambda, lru_norm,
           w_out, ln1_g, ln1_b, router_w, router_b, w_gate_up, b_gate_up, w_down, b_down, ln2_g, ln2_b):
    bsz, s, d = x.shape
    depth = w_in.shape[0]
    hg_w = hg_lb.shape[-1]
    lru_w = lru_lambda.shape[-1]
    n_exp = router_w.shape[-1]
    de = w_down.shape[2]
    t = bsz * s
    alpha = float((2 * depth) ** 0.25)
    block_rows = 2 * MXU_DIM
    assert w_in.shape[-1] == 5 * hg_w + 2 * lru_w and hg_w == lru_w
    assert s % 512 == 0 or s < 512 and s % HG_CHUNK == 0

    p = jax.nn.softmax(hg_lb.astype(F32), axis=0)
    lower_bounds = jnp.clip(jnp.cumsum(p, axis=0) - p[0:1], 0.0, 1.0 - 1e-6)
    w_in_b = w_in.astype(BF16)
    w_out_b = w_out.astype(BF16)
    wg = _block_diag_gate_weights(lru_wa, lru_wx)
    gate_bias = jnp.stack([lru_ba, lru_bx], axis=2).astype(F32)
    row3 = lambda a: a.astype(F32).reshape(depth, 1, a.shape[-1])
    pad_experts = lambda a: jnp.pad(a.astype(F32), [(0, 0)] * (a.ndim - 1) + [(0, LANES - n_exp)])
    router_w_pad = pad_experts(router_w)
    router_b_pad = pad_experts(row3(router_b))
    b_gu4 = b_gate_up.astype(F32).reshape(depth, n_exp, 1, 2 * de)
    b_dn4 = b_down.astype(F32).reshape(depth, n_exp, 1, d)

    x2 = x.reshape(t, d)
    for layer in range(depth):
        proj = _in_proj(x2, w_in_b, layer)
        proj3 = proj.reshape(bsz, s, proj.shape[-1])
        o_f, o_b = _hgrn2(proj3, lower_bounds, layer, hg_w)
        h_f, h_b = _griffin(proj3, lru_conv_w, row3(lru_conv_b), wg, gate_bias, lru_lambda, layer, lru_w,
                            col_block=5)
        x1, ridx, rgate, counts_f32 = _mix(x2, o_f.reshape(t, hg_w), o_b.reshape(t, hg_w), h_f.reshape(t, lru_w),
                               h_b.reshape(t, lru_w), proj, w_out_b, row3(hg_norm), row3(lru_norm), row3(ln1_g),
                               row3(ln1_b), router_w_pad, router_b_pad, n_exp, layer, alpha, zg_block=4, zy_block=6)
        dest, counts, pstarts, blk_e, n_used, blk_rows, n_rows = _routing_tables(ridx, counts_f32, n_exp, block_rows)
        xs = _dispatch(x1, dest, counts, pstarts, n_rows, block_rows, d)
        ys = _moe_ffn(xs.reshape(-1, LANES), blk_e, n_used, blk_rows, w_gate_up, b_gu4, w_down, b_dn4, layer,
                      block_rows)
        x2 = _combine(x1, ys.reshape(xs.shape), dest, rgate.T, row3(ln2_g), row3(ln2_b), layer, alpha)
    return x2.reshape(bsz, s, d)
```

```python
import functools

import jax
import jax.numpy as jnp
from jax import lax
from jax.experimental import pallas as pl
from jax.experimental.pallas import tpu as pltpu

F32 = jnp.float32
BF16 = jnp.bfloat16

HG_HEADS = 4
HG_CHUNK = 32
LB_FLOOR = 1e-30
LRU_HEADS = 8
LRU_C = 8.0
CONV_WIDTH = 4
TOP_K = 4
SWIGLU_LIMIT = 7.0
SWIGLU_ALPHA = 1.702
LN_EPS = 1e-5
RMS_EPS = 1e-6

LANES = 128
SUBLANES = 8
MXU_DIM = 256
VMEM_LIMIT_BYTES = 56 * 1024 * 1024

NT_DIMS = (((1,), (1,)), ((), ()))
TN_DIMS = (((0,), (0,)), ((), ()))


def _params(*semantics):
    return pltpu.CompilerParams(dimension_semantics=semantics, vmem_limit_bytes=VMEM_LIMIT_BYTES)


def _bdot(a, b):
    return jnp.dot(a.astype(BF16), b.astype(BF16), preferred_element_type=F32)


def _bdot_general(a, b, dims):
    return lax.dot_general(a.astype(BF16), b.astype(BF16), dims, preferred_element_type=F32)


def _layernorm(t, g, b):
    mu = jnp.mean(t, axis=-1, keepdims=True)
    c = t - mu
    var = jnp.mean(c * c, axis=-1, keepdims=True)
    return c * lax.rsqrt(var + LN_EPS) * g + b


def _load_row_tiles(ref, n_rows):
    per_row = ref.shape[0] // n_rows
    return jnp.concatenate([ref[pl.ds(j, n_rows, stride=per_row), :] for j in range(per_row)], axis=1)


def _store_row_tiles(ref, value):
    n_rows, d = value.shape
    per_row = d // LANES
    for j in range(per_row):
        ref[pl.ds(j, n_rows, stride=per_row), :] = value[:, j * LANES:(j + 1) * LANES]


def _in_proj_kernel(x_ref, w_ref, o_ref):
    o_ref[...] = jnp.dot(x_ref[...].astype(BF16), w_ref[...], preferred_element_type=F32)


def _in_proj(x2, w_in_bf16, layer):
    t, d = x2.shape
    n = w_in_bf16.shape[-1]
    tm = min(512, t)
    return pl.pallas_call(
        _in_proj_kernel,
        grid=(t // tm,),
        in_specs=[
            pl.BlockSpec((tm, d), lambda i: (i, 0)),
            pl.BlockSpec((None, d, n), lambda i: (layer, 0, 0)),
        ],
        out_specs=pl.BlockSpec((tm, n), lambda i: (i, 0)),
        out_shape=jax.ShapeDtypeStruct((t, n), F32),
        compiler_params=_params("parallel"),
        name="in_proj",
    )(x2, w_in_bf16)


def _cumsum_rows(x):
    n = x.shape[0]
    row = lax.broadcasted_iota(jnp.int32, x.shape, 0)
    s = 1
    while s < n:
        x = x + jnp.where(row >= s, pltpu.roll(x, s, 0), 0.0)
        s *= 2
    return x


def _hg_chunk(zq, zv, zf, lb, log_lb, log1m_lb, st_ref, o_ref, r0, reverse):
    c = HG_CHUNK
    q = zq * jax.nn.sigmoid(zq)
    e = jnp.exp(-jnp.abs(zf))
    log_sig = jnp.minimum(zf, 0.0) - jnp.log1p(e)
    sig_neg = jnp.where(zf >= 0.0, e, 1.0) / (1.0 + e)
    b = log1m_lb + log_sig
    logf = jnp.maximum(log_lb, b) + jnp.log1p(jnp.exp(-jnp.abs(log_lb - b)))
    k = (1.0 - lb) * sig_neg

    p = _cumsum_rows(logf)
    g_last = p[c - 1:c]
    if reverse:
        g = g_last - p + logf
        g_ref = g[c // 2:c // 2 + 1]
    else:
        g = p
        g_ref = g[c // 2 - 1:c // 2]
    qg = q * jnp.exp(g - g_ref)
    kg = k * jnp.exp(g_ref - g)
    kl = kg * jnp.exp(g_last - g_ref)
    qe = qg * jnp.exp(g_ref)
    dec = jnp.exp(g_last)

    row = lax.broadcasted_iota(jnp.int32, (c, c), 0)
    col = lax.broadcasted_iota(jnp.int32, (c, c), 1)
    keep = (col >= row) if reverse else (col <= row)
    hd = zq.shape[1] // HG_HEADS
    for h in range(HG_HEADS):
        sl = slice(h * hd, (h + 1) * hd)
        scores = jnp.where(keep, _bdot_general(qg[:, sl], kg[:, sl], NT_DIMS), 0.0)
        st = st_ref[h]
        o = _bdot(scores, zv[:, sl]) + _bdot_general(qe[:, sl], st, NT_DIMS)
        o_ref[pl.ds(r0, c), sl] = o
        st_ref[h] = st * dec[:, sl] + _bdot_general(zv[:, sl], kl[:, sl], TN_DIMS)


def _hgrn2_kernel(fqif_ref, bqi_ref, bzb_ref, lb_ref, of_ref, ob_ref, stf_ref, stb_ref):
    @pl.when(pl.program_id(1) == 0)
    def _():
        stf_ref[...] = jnp.zeros_like(stf_ref)
        stb_ref[...] = jnp.zeros_like(stb_ref)

    w = lb_ref.shape[1]
    lb = lb_ref[...]
    log_lb = jnp.log(jnp.maximum(lb, LB_FLOOR))
    log1m_lb = jnp.log1p(-lb)
    n_chunks = of_ref.shape[0] // HG_CHUNK

    def body(ci, carry):
        r0 = pl.multiple_of(ci * HG_CHUNK, HG_CHUNK)
        rows = pl.ds(r0, HG_CHUNK)
        _hg_chunk(fqif_ref[rows, 0:w], fqif_ref[rows, w:2 * w], fqif_ref[rows, 2 * w:3 * w],
                  lb[0:1], log_lb[0:1], log1m_lb[0:1], stf_ref, of_ref, r0, False)
        rb = pl.multiple_of((n_chunks - 1 - ci) * HG_CHUNK, HG_CHUNK)
        rows_b = pl.ds(rb, HG_CHUNK)
        _hg_chunk(bqi_ref[rows_b, 0:w], bqi_ref[rows_b, w:2 * w], bzb_ref[rows_b, :],
                  lb[1:2], log_lb[1:2], log1m_lb[1:2], stb_ref, ob_ref, rb, True)
        return carry

    lax.fori_loop(0, n_chunks, body, 0, unroll=8)


def _hgrn2(proj3, lower_bounds, layer, hg_width):
    bsz, s, _ = proj3.shape
    w = hg_width
    ts = min(512, s)
    ns = s // ts
    hd = w // HG_HEADS
    out = jax.ShapeDtypeStruct((bsz, s, w), F32)
    state = pltpu.VMEM((HG_HEADS, hd, hd), F32)
    return pl.pallas_call(
        _hgrn2_kernel,
        grid=(bsz, ns),
        in_specs=[
            pl.BlockSpec((None, ts, 3 * w), lambda b, j: (b, j, 0)),
            pl.BlockSpec((None, ts, 2 * w), lambda b, j: (b, ns - 1 - j, 0)),
            pl.BlockSpec((None, ts, w), lambda b, j: (b, ns - 1 - j, 3)),
            pl.BlockSpec((None, 2, w), lambda b, j: (layer, 0, 0)),
        ],
        out_specs=[
            pl.BlockSpec((None, ts, w), lambda b, j: (b, j, 0)),
            pl.BlockSpec((None, ts, w), lambda b, j: (b, ns - 1 - j, 0)),
        ],
        out_shape=[out, out],
        scratch_shapes=[state, state],
        compiler_params=_params("parallel", "arbitrary"),
        name="hgrn2",
    )(proj3, proj3, proj3, lower_bounds)


def _lin_scan(a, u, carry, h_ref, reverse):
    n, w = a.shape
    n_groups = n // SUBLANES
    a = a.reshape(n_groups, SUBLANES, w)
    u = u.reshape(n_groups, SUBLANES, w)
    sub = lax.broadcasted_iota(jnp.int32, a.shape, 1)
    s = 1
    while s < SUBLANES:
        shift = (SUBLANES - s) if reverse else s
        m = (sub < SUBLANES - s) if reverse else (sub >= s)
        a_sh = jnp.where(m, pltpu.roll(a, shift, 1), 1.0)
        u_sh = jnp.where(m, pltpu.roll(u, shift, 1), 0.0)
        u = u + a * u_sh
        a = a * a_sh
        s *= 2
    groups = range(n_groups)
    for g in (reversed(groups) if reverse else groups):
        hg = u[g] + a[g] * carry
        h_ref[g * SUBLANES:(g + 1) * SUBLANES, :] = hg
        carry = hg[0:1] if reverse else hg[SUBLANES - 1:SUBLANES]
    return carry


def _griffin_dir(main_ref, prev_ref, next_ref, is_first, is_last, cw, cb, wg_ref, bias, sp,
                 carry_ref, h_ref, reverse):
    ts, w = main_ref.shape
    ng = ts // SUBLANES
    groups = jnp.concatenate([jnp.where(is_first, 0.0, prev_ref[...]).reshape(1, SUBLANES, w),
                              main_ref[...].reshape(ng, SUBLANES, w),
                              jnp.where(is_last, 0.0, next_ref[...]).reshape(1, SUBLANES, w)], axis=0)
    sub = lax.broadcasted_iota(jnp.int32, (ng, SUBLANES, w), 1)
    down1 = pltpu.roll(groups, 1, 1)
    down2 = pltpu.roll(groups, 2, 1)
    up1 = pltpu.roll(groups, SUBLANES - 1, 1)
    x_m1 = jnp.where(sub >= 1, down1[1:ng + 1], down1[0:ng])
    x_m2 = jnp.where(sub >= 2, down2[1:ng + 1], down2[0:ng])
    x_p1 = jnp.where(sub < SUBLANES - 1, up1[1:ng + 1], up1[2:ng + 2])
    taps = (x_m2, x_m1, groups[1:ng + 1], x_p1)
    xc = cb.reshape(1, 1, w)
    for j in range(CONV_WIDTH):
        xc = xc + cw[j:j + 1].reshape(1, 1, w) * taps[j]
    xc = xc.reshape(ts, w)

    half = w // 2
    pre = [_bdot(xc[:, i * half:(i + 1) * half], wg_ref[i]) for i in range(2)]
    r_pre = jnp.concatenate([pre[0][:, :half], pre[1][:, :half]], axis=1) + bias[0:1]
    i_pre = jnp.concatenate([pre[0][:, half:], pre[1][:, half:]], axis=1) + bias[1:2]
    r = 0.5 + 0.5 * jnp.tanh(0.5 * r_pre)
    ig = 0.5 + 0.5 * jnp.tanh(0.5 * i_pre)
    log_a = (-LRU_C) * r * sp
    a = jnp.exp(log_a)
    u = jnp.sqrt(jnp.maximum(1.0 - a * a, 0.0)) * (ig * xc)
    carry_ref[0:1, :] = _lin_scan(a, u, carry_ref[0:1, :], h_ref, reverse)


def _griffin_kernel(fm_ref, fp_ref, fn_ref, bm_ref, bp_ref, bn_ref, cw_ref, cb_ref, wg_ref, bias_ref, lam_ref,
                    hf_ref, hb_ref, cf_ref, cbk_ref):
    j = pl.program_id(1)
    ns = pl.num_programs(1)

    @pl.when(j == 0)
    def _():
        cf_ref[...] = jnp.zeros_like(cf_ref)
        cbk_ref[...] = jnp.zeros_like(cbk_ref)

    cw = cw_ref[...]
    cb = cb_ref[...]
    sp = jax.nn.softplus(-lam_ref[...])
    _griffin_dir(fm_ref, fp_ref, fn_ref, j == 0, j == ns - 1, cw, cb, wg_ref.at[0], bias_ref[0], sp[0:1],
                 cf_ref, hf_ref, False)
    _griffin_dir(bm_ref, bp_ref, bn_ref, j == ns - 1, j == 0, cw, cb, wg_ref.at[1], bias_ref[1], sp[1:2],
                 cbk_ref, hb_ref, True)


def _griffin(proj3, conv_w, conv_b3, wg, gate_bias, lam, layer, lru_width, col_block):
    bsz, s, _ = proj3.shape
    w = lru_width
    ts = min(512, s)
    ns = s // ts
    tb = ts // SUBLANES
    nb8 = s // SUBLANES
    out = jax.ShapeDtypeStruct((bsz, s, w), F32)
    halo = (None, SUBLANES, w)
    return pl.pallas_call(
        _griffin_kernel,
        grid=(bsz, ns),
        in_specs=[
            pl.BlockSpec((None, ts, w), lambda b, j: (b, j, col_block)),
            pl.BlockSpec(halo, lambda b, j: (b, jnp.maximum(j * tb - 1, 0), col_block)),
            pl.BlockSpec(halo, lambda b, j: (b, jnp.minimum((j + 1) * tb, nb8 - 1), col_block)),
            pl.BlockSpec((None, ts, w), lambda b, j: (b, ns - 1 - j, col_block)),
            pl.BlockSpec(halo, lambda b, j: (b, jnp.maximum((ns - 1 - j) * tb - 1, 0), col_block)),
            pl.BlockSpec(halo, lambda b, j: (b, jnp.minimum((ns - j) * tb, nb8 - 1), col_block)),
            pl.BlockSpec((None, CONV_WIDTH, w), lambda b, j: (layer, 0, 0)),
            pl.BlockSpec((None, 1, w), lambda b, j: (layer, 0, 0)),
            pl.BlockSpec((None, 2, 2, w // 2, w), lambda b, j: (layer, 0, 0, 0, 0)),
            pl.BlockSpec((None, 2, 2, w), lambda b, j: (layer, 0, 0, 0)),
            pl.BlockSpec((None, 2, w), lambda b, j: (layer, 0, 0)),
        ],
        out_specs=[
            pl.BlockSpec((None, ts, w), lambda b, j: (b, j, 0)),
            pl.BlockSpec((None, ts, w), lambda b, j: (b, ns - 1 - j, 0)),
        ],
        out_shape=[out, out],
        scratch_shapes=[
            pltpu.VMEM((SUBLANES, w), F32),
            pltpu.VMEM((SUBLANES, w), F32),
        ],
        compiler_params=_params("parallel", "arbitrary"),
        name="griffin",
    )(proj3, proj3, proj3, proj3, proj3, proj3, conv_w, conv_b3, wg, gate_bias, lam)


def _gelu_tanh(x):
    return 0.5 * x * (1.0 + jnp.tanh(0.7978845608028654 * (x + 0.044715 * x * x * x)))


def _mix_kernel(x_ref, of_ref, ob_ref, zg_ref, hf_ref, hb_ref, zy_ref, wout_ref, hgn_ref, lrn_ref, g1_ref, b1_ref,
                rw_ref, rb_ref, tri_ref, x1_ref, ridx_ref, rgate_ref, counts_ref, cnt_ref, *, alpha, n_exp):
    @pl.when(pl.program_id(0) == 0)
    def _():
        cnt_ref[...] = jnp.zeros_like(cnt_ref)

    o = of_ref[...] + ob_ref[...]
    w = o.shape[1]
    hd = w // HG_HEADS
    parts = []
    for h in range(HG_HEADS):
        oh = o[:, h * hd:(h + 1) * hd]
        parts.append(oh * lax.rsqrt(jnp.mean(oh * oh, axis=-1, keepdims=True) + RMS_EPS))
    zg = zg_ref[...]
    o_hg = jnp.concatenate(parts, axis=1) * hgn_ref[...] * (zg * jax.nn.sigmoid(zg))
    hh = hf_ref[...] + hb_ref[...]
    o_lru = hh * lax.rsqrt(jnp.mean(hh * hh, axis=-1, keepdims=True) + RMS_EPS) * lrn_ref[...]
    o_lru = o_lru * _gelu_tanh(zy_ref[...])
    y = _bdot(o_hg, wout_ref[0:w, :]) + _bdot(o_lru, wout_ref[w:, :])
    x1 = _layernorm(alpha * x_ref[...] + y, g1_ref[...], b1_ref[...])
    _store_row_tiles(x1_ref, x1)

    rw = rw_ref[...]
    x_hi, w_hi = x1.astype(BF16), rw.astype(BF16)
    x_lo = (x1 - x_hi.astype(F32)).astype(BF16)
    w_lo = (rw - w_hi.astype(F32)).astype(BF16)
    n_pad = rw.shape[1]
    hi_both = jnp.dot(x_hi, jnp.concatenate([w_hi, w_lo], axis=1), preferred_element_type=F32)
    logits = (hi_both[:, :n_pad] + hi_both[:, n_pad:] + jnp.dot(x_lo, w_hi, preferred_element_type=F32)
              + rb_ref[...])

    lt = jnp.transpose(logits)[0:n_exp]
    eid = lax.broadcasted_iota(jnp.int32, lt.shape, 0)
    vals, idxs = [], []
    for _ in range(TOP_K):
        m = jnp.max(lt, axis=0, keepdims=True)
        idx = jnp.min(jnp.where(lt == m, eid, n_exp), axis=0, keepdims=True)
        vals.append(m)
        idxs.append(idx)
        lt = jnp.where(eid == idx, -jnp.inf, lt)
    exps = [jnp.exp(v - vals[0]) for v in vals]
    denom = exps[0] + exps[1] + exps[2] + exps[3]

    base = cnt_ref[:, 0:1]
    ranks = []
    for kk in range(TOP_K):
        onehot = (eid == idxs[kk]).astype(F32)
        before = jnp.dot(onehot.astype(BF16), tri_ref[...], preferred_element_type=F32) + base
        ranks.append(jnp.sum(onehot * before, axis=0, keepdims=True).astype(jnp.int32))
        base = base + jnp.sum(onehot, axis=1, keepdims=True)
    cnt_ref[...] = jnp.broadcast_to(base, cnt_ref.shape)
    counts_ref[...] = cnt_ref[...]

    out_row = lax.broadcasted_iota(jnp.int32, ridx_ref.shape, 0)
    ridx = jnp.zeros(ridx_ref.shape, jnp.int32)
    rgate = jnp.zeros(rgate_ref.shape, F32)
    for kk in range(TOP_K):
        ridx = jnp.where(out_row == kk, idxs[kk], ridx)
        ridx = jnp.where(out_row == TOP_K + kk, ranks[kk], ridx)
        rgate = jnp.where(out_row == kk, exps[kk] / denom, rgate)
    ridx_ref[...] = ridx
    rgate_ref[...] = rgate


def _mix(x2, o_f, o_b, h_f, h_b, proj, w_out_bf16, hg_norm3, lru_norm3, ln_g3, ln_b3, router_w_pad, router_b3_pad,
         n_exp, layer, alpha, zg_block, zy_block):
    t, d = x2.shape
    w = o_f.shape[1]
    assert router_w_pad.shape[-1] == LANES and n_exp <= LANES and n_exp % SUBLANES == 0
    tm = min(512, t)
    row_blk = lambda width: pl.BlockSpec((tm, width), lambda i: (i, 0))
    col_blk = pl.BlockSpec((2 * TOP_K, tm), lambda i: (0, i))
    vec = lambda width: pl.BlockSpec((None, 1, width), lambda i: (layer, 0, 0))
    return pl.pallas_call(
        functools.partial(_mix_kernel, alpha=alpha, n_exp=n_exp),
        grid=(t // tm,),
        in_specs=[
            row_blk(d), row_blk(w), row_blk(w),
            pl.BlockSpec((tm, w), lambda i: (i, zg_block)),
            row_blk(w), row_blk(w),
            pl.BlockSpec((tm, w), lambda i: (i, zy_block)),
            pl.BlockSpec((None, 2 * w, d), lambda i: (layer, 0, 0)),
            vec(w), vec(w), vec(d), vec(d),
            pl.BlockSpec((None, d, LANES), lambda i: (layer, 0, 0)),
            vec(LANES),
            pl.BlockSpec((tm, tm), lambda i: (0, 0)),
        ],
        out_specs=[pl.BlockSpec((tm * d // LANES, LANES), lambda i: (i, 0)), col_blk, col_blk,
                   pl.BlockSpec((n_exp, LANES), lambda i: (0, 0))],
        out_shape=[
            jax.ShapeDtypeStruct((t * d // LANES, LANES), F32),
            jax.ShapeDtypeStruct((2 * TOP_K, t), jnp.int32),
            jax.ShapeDtypeStruct((2 * TOP_K, t), F32),
            jax.ShapeDtypeStruct((n_exp, LANES), F32),
        ],
        scratch_shapes=[pltpu.VMEM((n_exp, LANES), F32)],
        compiler_params=_params("arbitrary"),
        name="mix_ln_router",
    )(x2, o_f, o_b, proj, h_f, h_b, proj, w_out_bf16, hg_norm3, lru_norm3, ln_g3, ln_b3, router_w_pad,
      router_b3_pad, jnp.triu(jnp.ones((tm, tm), BF16), k=1))


DMA_UNROLL_ROWS = 32


def _routing_tables(ridx, counts_f32, n_exp, block_rows):
    t = ridx.shape[1]
    idx = ridx[:TOP_K].T
    rank = ridx[TOP_K:2 * TOP_K].T
    counts = counts_f32[:, 0].astype(jnp.int32)
    padded = ((counts + block_rows - 1) // block_rows) * block_rows
    pends = jnp.cumsum(padded)
    pstarts = pends - padded
    onehot = idx[:, :, None] == jnp.arange(n_exp, dtype=jnp.int32)[None, None, :]
    dest = (jnp.sum(jnp.where(onehot, pstarts[None, None, :], 0), axis=-1) + rank).reshape(-1).astype(jnp.int32)
    n_rows = t * TOP_K + n_exp * block_rows
    n_blocks = n_rows // block_rows
    blk_start = jnp.arange(n_blocks, dtype=jnp.int32) * block_rows
    blk_e = jnp.minimum(jnp.searchsorted(pends, blk_start, side="right", method="compare_all"),
                        n_exp - 1).astype(jnp.int32)
    n_used = (pends[-1] // block_rows).astype(jnp.int32).reshape(1)
    return dest, counts, pstarts.astype(jnp.int32), blk_e, n_used, n_rows


def _dispatch_kernel(cnt_ref, pst_ref, dest_ref, x_ref, xs_ref, zero_ref, sem, *, block_rows):
    per_row = xs_ref.shape[1]
    tm = x_ref.shape[0] // per_row

    def issue(g, carry):
        r0 = pl.multiple_of(g * DMA_UNROLL_ROWS, DMA_UNROLL_ROWS)
        for j in range(DMA_UNROLL_ROWS):
            src = x_ref.at[pl.ds(pl.multiple_of((r0 + j) * per_row, per_row), per_row)]
            for kk in range(TOP_K):
                d = dest_ref[(r0 + j) * TOP_K + kk]
                pltpu.make_async_copy(src, xs_ref.at[d], sem).start(priority=kk % 2)
        return carry

    lax.fori_loop(0, tm // DMA_UNROLL_ROWS, issue, 0)
    for _ in range(TOP_K):
        pltpu.make_async_copy(xs_ref.at[pl.ds(0, tm)], xs_ref.at[pl.ds(0, tm)], sem).wait()

    @pl.when(pl.program_id(0) == pl.num_programs(0) - 1)
    def _():
        zero_ref[...] = jnp.zeros_like(zero_ref)
        n_exp = cnt_ref.shape[0]
        last = n_exp - 1
        cnt_last = cnt_ref[last]
        used_rows = pst_ref[last] + cnt_last + (block_rows - cnt_last % block_rows) % block_rows
        n_tail = (xs_ref.shape[0] - used_rows) // block_rows

        def tail_copy(b):
            start = pl.multiple_of(used_rows + b * block_rows, block_rows)
            return pltpu.make_async_copy(zero_ref, xs_ref.at[pl.ds(start, block_rows)], sem)

        def tail_start(b, c2):
            tail_copy(b).start()
            return c2

        def tail_wait(b, c2):
            tail_copy(b).wait()
            return c2

        lax.fori_loop(0, n_tail, tail_start, 0)
        lax.fori_loop(0, n_tail, tail_wait, 0)

        def per_expert(e, carry):
            cnt = cnt_ref[e]
            first = pst_ref[e] + cnt
            n_pad = (block_rows - cnt % block_rows) % block_rows

            def pieces(action):
                off = first
                for bit in range(block_rows.bit_length() - 1):
                    size = 1 << bit
                    copy = pltpu.make_async_copy(zero_ref.at[pl.ds(0, size)], xs_ref.at[pl.ds(off, size)], sem)
                    pl.when((n_pad & size) != 0)(functools.partial(action, copy))
                    off = off + (n_pad & size)

            pieces(lambda copy: copy.start())
            pieces(lambda copy: copy.wait())
            return carry

        lax.fori_loop(0, n_exp, per_expert, 0)


def _dispatch(x1_tiles, dest, counts, pstarts, n_rows, block_rows, d):
    per_row = d // LANES
    t = x1_tiles.shape[0] // per_row
    tm = min(512, t)
    assert tm % DMA_UNROLL_ROWS == 0 and block_rows & (block_rows - 1) == 0
    return pl.pallas_call(
        functools.partial(_dispatch_kernel, block_rows=block_rows),
        grid_spec=pltpu.PrefetchScalarGridSpec(
            num_scalar_prefetch=2,
            grid=(t // tm,),
            in_specs=[
                pl.BlockSpec((tm * TOP_K,), lambda i, c, p: (i,), memory_space=pltpu.SMEM),
                pl.BlockSpec((tm * per_row, LANES), lambda i, c, p: (i, 0)),
            ],
            out_specs=pl.BlockSpec(memory_space=pl.ANY),
            scratch_shapes=[pltpu.VMEM((block_rows, per_row, LANES), F32), pltpu.SemaphoreType.DMA],
        ),
        out_shape=jax.ShapeDtypeStruct((n_rows, per_row, LANES), F32),
        compiler_params=_params("arbitrary"),
        name="moe_dispatch",
    )(counts, pstarts, dest, x1_tiles)


def _moe_ffn_kernel(blk_e_ref, n_used_ref, xs_ref, wgu_f32_ref, bgu_ref, wdn_f32_ref, bdn_ref, ys_ref,
                    wgu_ref, wdn_ref):
    i = pl.program_id(0)
    used = i < n_used_ref[0]

    @pl.when(used & ((i == 0) | (blk_e_ref[i] != blk_e_ref[jnp.maximum(i - 1, 0)])))
    def _():
        rows = LANES

        def cast_gu(j, carry):
            sl = pl.ds(pl.multiple_of(j * rows, rows), rows)
            wgu_ref[sl, :] = wgu_f32_ref[sl, :].astype(BF16)
            return carry

        def cast_dn(j, carry):
            sl = pl.ds(pl.multiple_of(j * rows, rows), rows)
            wdn_ref[sl, :] = wdn_f32_ref[sl, :].astype(BF16)
            return carry

        lax.fori_loop(0, wgu_ref.shape[0] // rows, cast_gu, 0)
        lax.fori_loop(0, wdn_ref.shape[0] // rows, cast_dn, 0)

    @pl.when(used)
    def _():
        de = wdn_ref.shape[0]
        bm = xs_ref.shape[0] * LANES // wgu_ref.shape[0]
        x = _load_row_tiles(xs_ref, bm).astype(BF16)
        gu = jnp.dot(x, wgu_ref[...], preferred_element_type=F32) + bgu_ref[...]
        gate = jnp.minimum(gu[:, :de], SWIGLU_LIMIT)
        up = jnp.clip(gu[:, de:], -SWIGLU_LIMIT, SWIGLU_LIMIT)
        hid = (up + 1.0) * (gate * jax.nn.sigmoid(SWIGLU_ALPHA * gate))
        _store_row_tiles(ys_ref, jnp.dot(hid.astype(BF16), wdn_ref[...], preferred_element_type=F32) + bdn_ref[...])

    @pl.when(jnp.logical_not(used))
    def _():
        ys_ref[...] = jnp.zeros_like(ys_ref)


def _moe_ffn(xs_tiles, blk_e, n_used, w_gate_up, b_gu4, w_down, b_dn4, layer, block_rows):
    d, de = w_down.shape[3], w_down.shape[2]
    per_row = d // LANES
    n_blocks = xs_tiles.shape[0] // (block_rows * per_row)
    w_map = lambda i, be, nu: (layer, be[i], 0, 0)
    return pl.pallas_call(
        _moe_ffn_kernel,
        grid_spec=pltpu.PrefetchScalarGridSpec(
            num_scalar_prefetch=2,
            grid=(n_blocks,),
            in_specs=[
                pl.BlockSpec((block_rows * per_row, LANES), lambda i, be, nu: (jnp.minimum(i, nu[0] - 1), 0)),
                pl.BlockSpec((None, None, d, 2 * de), w_map),
                pl.BlockSpec((None, None, 1, 2 * de), w_map),
                pl.BlockSpec((None, None, de, d), w_map),
                pl.BlockSpec((None, None, 1, d), w_map),
            ],
            out_specs=pl.BlockSpec((block_rows * per_row, LANES), lambda i, be, nu: (i, 0)),
            scratch_shapes=[pltpu.VMEM((d, 2 * de), BF16), pltpu.VMEM((de, d), BF16)],
        ),
        out_shape=jax.ShapeDtypeStruct(xs_tiles.shape, F32),
        compiler_params=_params("arbitrary"),
        name="moe_ffn",
    )(blk_e, n_used, xs_tiles, w_gate_up, b_gu4, w_down, b_dn4)


def _combine_kernel(dest_ref, gate_ref, x1_ref, ys_ref, g2_ref, b2_ref, o_ref, buf0, buf1, sem, *, alpha):
    i = pl.program_id(0)
    n_tiles = pl.num_programs(0) - 1
    tm = o_ref.shape[0]
    per_row = ys_ref.shape[1]

    def issue(buf, slot):
        def body(g, carry):
            r0 = pl.multiple_of(g * DMA_UNROLL_ROWS, DMA_UNROLL_ROWS)
            for j in range(DMA_UNROLL_ROWS):
                rows = pl.ds(pl.multiple_of((r0 + j) * per_row, per_row), per_row)
                for kk in range(TOP_K):
                    src = dest_ref[(r0 + j) * TOP_K + kk]
                    pltpu.make_async_copy(ys_ref.at[src], buf.at[kk, rows], sem.at[slot]).start(priority=kk % 2)
            return carry

        lax.fori_loop(0, tm // DMA_UNROLL_ROWS, body, 0)

    def finish(buf, slot):
        for kk in range(TOP_K):
            pltpu.make_async_copy(ys_ref.at[pl.ds(0, tm)], ys_ref.at[pl.ds(0, tm)], sem.at[slot]).wait()
        gates = gate_ref[...]
        m = gates[:, 0:1] * _load_row_tiles(buf.at[0], tm)
        for kk in range(1, TOP_K):
            m = m + gates[:, kk:kk + 1] * _load_row_tiles(buf.at[kk], tm)
        o_ref[...] = _layernorm(alpha * _load_row_tiles(x1_ref, tm) + m, g2_ref[...], b2_ref[...])

    for parity, (cur, prev) in enumerate(((buf0, buf1), (buf1, buf0))):
        @pl.when(i % 2 == parity)
        def _():
            @pl.when(i < n_tiles)
            def _():
                issue(cur, parity)

            @pl.when(i > 0)
            def _():
                finish(prev, 1 - parity)


def _combine(x1_tiles, ys3, dest, rgate, ln_g3, ln_b3, layer, alpha):
    per_row = ys3.shape[1]
    d = per_row * LANES
    t = x1_tiles.shape[0] // per_row
    tm = min(256, t)
    nt = t // tm
    assert tm % DMA_UNROLL_ROWS == 0
    done = lambda i: (jnp.maximum(i - 1, 0), 0)
    row_buffer = pltpu.VMEM((TOP_K, tm * per_row, LANES), F32)
    return pl.pallas_call(
        functools.partial(_combine_kernel, alpha=alpha),
        grid=(nt + 1,),
        in_specs=[
            pl.BlockSpec((tm * TOP_K,), lambda i: (jnp.minimum(i, nt - 1),), memory_space=pltpu.SMEM),
            pl.BlockSpec((tm, rgate.shape[1]), done),
            pl.BlockSpec((tm * per_row, LANES), done),
            pl.BlockSpec(memory_space=pl.ANY),
            pl.BlockSpec((None, 1, d), lambda i: (layer, 0, 0)),
            pl.BlockSpec((None, 1, d), lambda i: (layer, 0, 0)),
        ],
        out_specs=pl.BlockSpec((tm, d), done),
        out_shape=jax.ShapeDtypeStruct((t, d), F32),
        scratch_shapes=[row_buffer, row_buffer, pltpu.SemaphoreType.DMA((2,))],
        compiler_params=_params("arbitrary"),
        name="moe_combine",
    )(dest, rgate, x1_tiles, ys3, ln_g3, ln_b3)


def _block_diag_gate_weights(wa, wx):
    n_l, n_dir, n_h, hd, _ = wa.shape
    hh = n_h // 2
    eye = jnp.eye(hh, dtype=wa.dtype)

    def bd(wm):
        wm = wm.reshape(n_l, n_dir, 2, hh, hd, hd)
        full = jnp.einsum("ldghij,hk->ldghikj", wm, eye)
        return full.reshape(n_l, n_dir, 2, hh * hd, hh * hd)

    return jnp.concatenate([bd(wa), bd(wx)], axis=-1).astype(BF16)


def kernel(x, w_in, hg_lb, hg_norm, lru_conv_w, lru_conv_b, lru_wa, lru_ba, lru_wx, lru_bx, lru_lambda, lru_norm,
           w_out, ln1_g, ln1_b, router_w, router_b, w_gate_up, b_gate_up, w_down, b_down, ln2_g, ln2_b):
    bsz, s, d = x.shape
    depth = w_in.shape[0]
    hg_w = hg_lb.shape[-1]
    lru_w = lru_lambda.shape[-1]
    n_exp = router_w.shape[-1]
    de = w_down.shape[2]
    t = bsz * s
    alpha = float((2 * depth) ** 0.25)
    block_rows = 2 * MXU_DIM
    assert w_in.shape[-1] == 5 * hg_w + 2 * lru_w and hg_w == lru_w
    assert s % 512 == 0 or s < 512 and s % HG_CHUNK == 0

    p = jax.nn.softmax(hg_lb.astype(F32), axis=0)
    lower_bounds = jnp.clip(jnp.cumsum(p, axis=0) - p[0:1], 0.0, 1.0 - 1e-6)
    w_in_b = w_in.astype(BF16)
    w_out_b = w_out.astype(BF16)
    wg = _block_diag_gate_weights(lru_wa, lru_wx)
    gate_bias = jnp.stack([lru_ba, lru_bx], axis=2).astype(F32)
    row3 = lambda a: a.astype(F32).reshape(depth, 1, a.shape[-1])
    pad_experts = lambda a: jnp.pad(a.astype(F32), [(0, 0)] * (a.ndim - 1) + [(0, LANES - n_exp)])
    router_w_pad = pad_experts(router_w)
    router_b_pad = pad_experts(row3(router_b))
    b_gu4 = b_gate_up.astype(F32).reshape(depth, n_exp, 1, 2 * de)
    b_dn4 = b_down.astype(F32).reshape(depth, n_exp, 1, d)

    x2 = x.reshape(t, d)
    for layer in range(depth):
        proj = _in_proj(x2, w_in_b, layer)
        proj3 = proj.reshape(bsz, s, proj.shape[-1])
        o_f, o_b = _hgrn2(proj3, lower_bounds, layer, hg_w)
        h_f, h_b = _griffin(proj3, lru_conv_w, row3(lru_conv_b), wg, gate_bias, lru_lambda, layer, lru_w,
                            col_block=5)
        x1, ridx, rgate, counts_f32 = _mix(x2, o_f.reshape(t, hg_w), o_b.reshape(t, hg_w), h_f.reshape(t, lru_w),
                               h_b.reshape(t, lru_w), proj, w_out_b, row3(hg_norm), row3(lru_norm), row3(ln1_g),
                               row3(ln1_b), router_w_pad, router_b_pad, n_exp, layer, alpha, zg_block=4, zy_block=6)
        dest, counts, pstarts, blk_e, n_used, n_rows = _routing_tables(ridx, counts_f32, n_exp, block_rows)
        xs = _dispatch(x1, dest, counts, pstarts, n_rows, block_rows, d)
        ys = _moe_ffn(xs.reshape(-1, LANES), blk_e, n_used, w_gate_up, b_gu4, w_down, b_dn4, layer, block_rows)
        x2 = _combine(x1, ys.reshape(xs.shape), dest, rgate.T, row3(ln2_g), row3(ln2_b), layer, alpha)
    return x2.reshape(bsz, s, d)
```

```python
import functools

import jax
import jax.numpy as jnp
from jax import lax
from jax.experimental import pallas as pl
from jax.experimental.pallas import tpu as pltpu

F32 = jnp.float32
BF16 = jnp.bfloat16

HG_HEADS = 4
HG_CHUNK = 32
LB_FLOOR = 1e-30
LRU_HEADS = 8
LRU_C = 8.0
CONV_WIDTH = 4
TOP_K = 4
SWIGLU_LIMIT = 7.0
SWIGLU_ALPHA = 1.702
LN_EPS = 1e-5
RMS_EPS = 1e-6

LANES = 128
SUBLANES = 8
MXU_DIM = 256
VMEM_LIMIT_BYTES = 56 * 1024 * 1024

NT_DIMS = (((1,), (1,)), ((), ()))
TN_DIMS = (((0,), (0,)), ((), ()))


def _params(*semantics):
    return pltpu.CompilerParams(dimension_semantics=semantics, vmem_limit_bytes=VMEM_LIMIT_BYTES)


def _bdot(a, b):
    return jnp.dot(a.astype(BF16), b.astype(BF16), preferred_element_type=F32)


def _bdot_general(a, b, dims):
    return lax.dot_general(a.astype(BF16), b.astype(BF16), dims, preferred_element_type=F32)


def _layernorm(t, g, b):
    mu = jnp.mean(t, axis=-1, keepdims=True)
    c = t - mu
    var = jnp.mean(c * c, axis=-1, keepdims=True)
    return c * lax.rsqrt(var + LN_EPS) * g + b


def _load_row_tiles(ref, n_rows):
    per_row = ref.shape[0] // n_rows
    return jnp.concatenate([ref[pl.ds(j, n_rows, stride=per_row), :] for j in range(per_row)], axis=1)


def _store_row_tiles(ref, value):
    n_rows, d = value.shape
    per_row = d // LANES
    for j in range(per_row):
        ref[pl.ds(j, n_rows, stride=per_row), :] = value[:, j * LANES:(j + 1) * LANES]


def _in_proj_kernel(x_ref, w_ref, o_ref):
    o_ref[...] = jnp.dot(x_ref[...].astype(BF16), w_ref[...], preferred_element_type=F32)


def _in_proj(x2, w_in_bf16, layer):
    t, d = x2.shape
    n = w_in_bf16.shape[-1]
    tm = min(512, t)
    return pl.pallas_call(
        _in_proj_kernel,
        grid=(t // tm,),
        in_specs=[
            pl.BlockSpec((tm, d), lambda i: (i, 0)),
            pl.BlockSpec((None, d, n), lambda i: (layer, 0, 0)),
        ],
        out_specs=pl.BlockSpec((tm, n), lambda i: (i, 0)),
        out_shape=jax.ShapeDtypeStruct((t, n), F32),
        compiler_params=_params("parallel"),
        name="in_proj",
    )(x2, w_in_bf16)


def _cumsum_rows(x):
    n = x.shape[0]
    row = lax.broadcasted_iota(jnp.int32, x.shape, 0)
    s = 1
    while s < n:
        x = x + jnp.where(row >= s, pltpu.roll(x, s, 0), 0.0)
        s *= 2
    return x


def _hg_chunk(zq, zv, zf, lb, log_lb, log1m_lb, st_ref, o_ref, r0, reverse):
    c = HG_CHUNK
    q = zq * jax.nn.sigmoid(zq)
    e = jnp.exp(-jnp.abs(zf))
    log_sig = jnp.minimum(zf, 0.0) - jnp.log1p(e)
    sig_neg = jnp.where(zf >= 0.0, e, 1.0) / (1.0 + e)
    b = log1m_lb + log_sig
    logf = jnp.maximum(log_lb, b) + jnp.log1p(jnp.exp(-jnp.abs(log_lb - b)))
    k = (1.0 - lb) * sig_neg

    p = _cumsum_rows(logf)
    g_last = p[c - 1:c]
    if reverse:
        g = g_last - p + logf
        g_ref = g[c // 2:c // 2 + 1]
    else:
        g = p
        g_ref = g[c // 2 - 1:c // 2]
    qg = q * jnp.exp(g - g_ref)
    kg = k * jnp.exp(g_ref - g)
    kl = kg * jnp.exp(g_last - g_ref)
    qe = qg * jnp.exp(g_ref)
    dec = jnp.exp(g_last)

    row = lax.broadcasted_iota(jnp.int32, (c, c), 0)
    col = lax.broadcasted_iota(jnp.int32, (c, c), 1)
    keep = (col >= row) if reverse else (col <= row)
    hd = zq.shape[1] // HG_HEADS
    for h in range(HG_HEADS):
        sl = slice(h * hd, (h + 1) * hd)
        scores = jnp.where(keep, _bdot_general(qg[:, sl], kg[:, sl], NT_DIMS), 0.0)
        st = st_ref[h]
        o = _bdot(scores, zv[:, sl]) + _bdot_general(qe[:, sl], st, NT_DIMS)
        o_ref[pl.ds(r0, c), sl] = o
        st_ref[h] = st * dec[:, sl] + _bdot_general(zv[:, sl], kl[:, sl], TN_DIMS)


def _hgrn2_kernel(fqif_ref, bqi_ref, bzb_ref, lb_ref, of_ref, ob_ref, stf_ref, stb_ref):
    @pl.when(pl.program_id(1) == 0)
    def _():
        stf_ref[...] = jnp.zeros_like(stf_ref)
        stb_ref[...] = jnp.zeros_like(stb_ref)

    w = lb_ref.shape[1]
    lb = lb_ref[...]
    log_lb = jnp.log(jnp.maximum(lb, LB_FLOOR))
    log1m_lb = jnp.log1p(-lb)
    n_chunks = of_ref.shape[0] // HG_CHUNK

    def body(ci, carry):
        r0 = pl.multiple_of(ci * HG_CHUNK, HG_CHUNK)
        rows = pl.ds(r0, HG_CHUNK)
        _hg_chunk(fqif_ref[rows, 0:w], fqif_ref[rows, w:2 * w], fqif_ref[rows, 2 * w:3 * w],
                  lb[0:1], log_lb[0:1], log1m_lb[0:1], stf_ref, of_ref, r0, False)
        rb = pl.multiple_of((n_chunks - 1 - ci) * HG_CHUNK, HG_CHUNK)
        rows_b = pl.ds(rb, HG_CHUNK)
        _hg_chunk(bqi_ref[rows_b, 0:w], bqi_ref[rows_b, w:2 * w], bzb_ref[rows_b, :],
                  lb[1:2], log_lb[1:2], log1m_lb[1:2], stb_ref, ob_ref, rb, True)
        return carry

    lax.fori_loop(0, n_chunks, body, 0, unroll=16)


def _hgrn2(proj3, lower_bounds, layer, hg_width):
    bsz, s, _ = proj3.shape
    w = hg_width
    ts = min(512, s)
    ns = s // ts
    hd = w // HG_HEADS
    out = jax.ShapeDtypeStruct((bsz, s, w), F32)
    state = pltpu.VMEM((HG_HEADS, hd, hd), F32)
    return pl.pallas_call(
        _hgrn2_kernel,
        grid=(bsz, ns),
        in_specs=[
            pl.BlockSpec((None, ts, 3 * w), lambda b, j: (b, j, 0)),
            pl.BlockSpec((None, ts, 2 * w), lambda b, j: (b, ns - 1 - j, 0)),
            pl.BlockSpec((None, ts, w), lambda b, j: (b, ns - 1 - j, 3)),
            pl.BlockSpec((None, 2, w), lambda b, j: (layer, 0, 0)),
        ],
        out_specs=[
            pl.BlockSpec((None, ts, w), lambda b, j: (b, j, 0)),
            pl.BlockSpec((None, ts, w), lambda b, j: (b, ns - 1 - j, 0)),
        ],
        out_shape=[out, out],
        scratch_shapes=[state, state],
        compiler_params=_params("parallel", "arbitrary"),
        name="hgrn2",
    )(proj3, proj3, proj3, lower_bounds)


def _lin_scan(a, u, carry, h_ref, reverse):
    n, w = a.shape
    n_groups = n // SUBLANES
    a = a.reshape(n_groups, SUBLANES, w)
    u = u.reshape(n_groups, SUBLANES, w)
    sub = lax.broadcasted_iota(jnp.int32, a.shape, 1)
    s = 1
    while s < SUBLANES:
        shift = (SUBLANES - s) if reverse else s
        m = (sub < SUBLANES - s) if reverse else (sub >= s)
        a_sh = jnp.where(m, pltpu.roll(a, shift, 1), 1.0)
        u_sh = jnp.where(m, pltpu.roll(u, shift, 1), 0.0)
        u = u + a * u_sh
        a = a * a_sh
        s *= 2
    groups = range(n_groups)
    for g in (reversed(groups) if reverse else groups):
        hg = u[g] + a[g] * carry
        h_ref[g * SUBLANES:(g + 1) * SUBLANES, :] = hg
        carry = hg[0:1] if reverse else hg[SUBLANES - 1:SUBLANES]
    return carry


def _griffin_dir(main_ref, prev_ref, next_ref, is_first, is_last, cw, cb, wg_ref, bias, sp,
                 carry_ref, h_ref, reverse):
    ts, w = main_ref.shape
    ng = ts // SUBLANES
    groups = jnp.concatenate([jnp.where(is_first, 0.0, prev_ref[...]).reshape(1, SUBLANES, w),
                              main_ref[...].reshape(ng, SUBLANES, w),
                              jnp.where(is_last, 0.0, next_ref[...]).reshape(1, SUBLANES, w)], axis=0)
    sub = lax.broadcasted_iota(jnp.int32, (ng, SUBLANES, w), 1)
    down1 = pltpu.roll(groups, 1, 1)
    down2 = pltpu.roll(groups, 2, 1)
    up1 = pltpu.roll(groups, SUBLANES - 1, 1)
    x_m1 = jnp.where(sub >= 1, down1[1:ng + 1], down1[0:ng])
    x_m2 = jnp.where(sub >= 2, down2[1:ng + 1], down2[0:ng])
    x_p1 = jnp.where(sub < SUBLANES - 1, up1[1:ng + 1], up1[2:ng + 2])
    taps = (x_m2, x_m1, groups[1:ng + 1], x_p1)
    xc = cb.reshape(1, 1, w)
    for j in range(CONV_WIDTH):
        xc = xc + cw[j:j + 1].reshape(1, 1, w) * taps[j]
    xc = xc.reshape(ts, w)

    half = w // 2
    pre = [_bdot(xc[:, i * half:(i + 1) * half], wg_ref[i]) for i in range(2)]
    r_pre = jnp.concatenate([pre[0][:, :half], pre[1][:, :half]], axis=1) + bias[0:1]
    i_pre = jnp.concatenate([pre[0][:, half:], pre[1][:, half:]], axis=1) + bias[1:2]
    r = 0.5 + 0.5 * jnp.tanh(0.5 * r_pre)
    ig = 0.5 + 0.5 * jnp.tanh(0.5 * i_pre)
    log_a = (-LRU_C) * r * sp
    a = jnp.exp(log_a)
    u = jnp.sqrt(jnp.maximum(1.0 - a * a, 0.0)) * (ig * xc)
    carry_ref[0:1, :] = _lin_scan(a, u, carry_ref[0:1, :], h_ref, reverse)


def _griffin_kernel(fm_ref, fp_ref, fn_ref, bm_ref, bp_ref, bn_ref, cw_ref, cb_ref, wg_ref, bias_ref, lam_ref,
                    hf_ref, hb_ref, cf_ref, cbk_ref):
    j = pl.program_id(1)
    ns = pl.num_programs(1)

    @pl.when(j == 0)
    def _():
        cf_ref[...] = jnp.zeros_like(cf_ref)
        cbk_ref[...] = jnp.zeros_like(cbk_ref)

    cw = cw_ref[...]
    cb = cb_ref[...]
    sp = jax.nn.softplus(-lam_ref[...])
    _griffin_dir(fm_ref, fp_ref, fn_ref, j == 0, j == ns - 1, cw, cb, wg_ref.at[0], bias_ref[0], sp[0:1],
                 cf_ref, hf_ref, False)
    _griffin_dir(bm_ref, bp_ref, bn_ref, j == ns - 1, j == 0, cw, cb, wg_ref.at[1], bias_ref[1], sp[1:2],
                 cbk_ref, hb_ref, True)


def _griffin(proj3, conv_w, conv_b3, wg, gate_bias, lam, layer, lru_width, col_block):
    bsz, s, _ = proj3.shape
    w = lru_width
    ts = min(512, s)
    ns = s // ts
    tb = ts // SUBLANES
    nb8 = s // SUBLANES
    out = jax.ShapeDtypeStruct((bsz, s, w), F32)
    halo = (None, SUBLANES, w)
    return pl.pallas_call(
        _griffin_kernel,
        grid=(bsz, ns),
        in_specs=[
            pl.BlockSpec((None, ts, w), lambda b, j: (b, j, col_block)),
            pl.BlockSpec(halo, lambda b, j: (b, jnp.maximum(j * tb - 1, 0), col_block)),
            pl.BlockSpec(halo, lambda b, j: (b, jnp.minimum((j + 1) * tb, nb8 - 1), col_block)),
            pl.BlockSpec((None, ts, w), lambda b, j: (b, ns - 1 - j, col_block)),
            pl.BlockSpec(halo, lambda b, j: (b, jnp.maximum((ns - 1 - j) * tb - 1, 0), col_block)),
            pl.BlockSpec(halo, lambda b, j: (b, jnp.minimum((ns - j) * tb, nb8 - 1), col_block)),
            pl.BlockSpec((None, CONV_WIDTH, w), lambda b, j: (layer, 0, 0)),
            pl.BlockSpec((None, 1, w), lambda b, j: (layer, 0, 0)),
            pl.BlockSpec((None, 2, 2, w // 2, w), lambda b, j: (layer, 0, 0, 0, 0)),
            pl.BlockSpec((None, 2, 2, w), lambda b, j: (layer, 0, 0, 0)),
            pl.BlockSpec((None, 2, w), lambda b, j: (layer, 0, 0)),
        ],
        out_specs=[
            pl.BlockSpec((None, ts, w), lambda b, j: (b, j, 0)),
            pl.BlockSpec((None, ts, w), lambda b, j: (b, ns - 1 - j, 0)),
        ],
        out_shape=[out, out],
        scratch_shapes=[
            pltpu.VMEM((SUBLANES, w), F32),
            pltpu.VMEM((SUBLANES, w), F32),
        ],
        compiler_params=_params("parallel", "arbitrary"),
        name="griffin",
    )(proj3, proj3, proj3, proj3, proj3, proj3, conv_w, conv_b3, wg, gate_bias, lam)


def _gelu_tanh(x):
    return 0.5 * x * (1.0 + jnp.tanh(0.7978845608028654 * (x + 0.044715 * x * x * x)))


def _mix_kernel(x_ref, of_ref, ob_ref, zg_ref, hf_ref, hb_ref, zy_ref, wout_ref, hgn_ref, lrn_ref, g1_ref, b1_ref,
                rw_ref, rb_ref, tri_ref, x1_ref, ridx_ref, rgate_ref, counts_ref, cnt_ref, *, alpha, n_exp):
    @pl.when(pl.program_id(0) == 0)
    def _():
        cnt_ref[...] = jnp.zeros_like(cnt_ref)

    o = of_ref[...] + ob_ref[...]
    w = o.shape[1]
    hd = w // HG_HEADS
    parts = []
    for h in range(HG_HEADS):
        oh = o[:, h * hd:(h + 1) * hd]
        parts.append(oh * lax.rsqrt(jnp.mean(oh * oh, axis=-1, keepdims=True) + RMS_EPS))
    zg = zg_ref[...]
    o_hg = jnp.concatenate(parts, axis=1) * hgn_ref[...] * (zg * jax.nn.sigmoid(zg))
    hh = hf_ref[...] + hb_ref[...]
    o_lru = hh * lax.rsqrt(jnp.mean(hh * hh, axis=-1, keepdims=True) + RMS_EPS) * lrn_ref[...]
    o_lru = o_lru * _gelu_tanh(zy_ref[...])
    y = _bdot(o_hg, wout_ref[0:w, :]) + _bdot(o_lru, wout_ref[w:, :])
    x1 = _layernorm(alpha * x_ref[...] + y, g1_ref[...], b1_ref[...])
    _store_row_tiles(x1_ref, x1)

    rw = rw_ref[...]
    x_hi, w_hi = x1.astype(BF16), rw.astype(BF16)
    x_lo = (x1 - x_hi.astype(F32)).astype(BF16)
    w_lo = (rw - w_hi.astype(F32)).astype(BF16)
    n_pad = rw.shape[1]
    hi_both = jnp.dot(x_hi, jnp.concatenate([w_hi, w_lo], axis=1), preferred_element_type=F32)
    logits = (hi_both[:, :n_pad] + hi_both[:, n_pad:] + jnp.dot(x_lo, w_hi, preferred_element_type=F32)
              + rb_ref[...])

    lt = jnp.transpose(logits)[0:n_exp]
    eid = lax.broadcasted_iota(jnp.int32, lt.shape, 0)
    vals, idxs = [], []
    for _ in range(TOP_K):
        m = jnp.max(lt, axis=0, keepdims=True)
        idx = jnp.min(jnp.where(lt == m, eid, n_exp), axis=0, keepdims=True)
        vals.append(m)
        idxs.append(idx)
        lt = jnp.where(eid == idx, -jnp.inf, lt)
    exps = [jnp.exp(v - vals[0]) for v in vals]
    denom = exps[0] + exps[1] + exps[2] + exps[3]

    base = cnt_ref[:, 0:1]
    ranks = []
    for kk in range(TOP_K):
        onehot = (eid == idxs[kk]).astype(F32)
        before = jnp.dot(onehot.astype(BF16), tri_ref[...], preferred_element_type=F32) + base
        ranks.append(jnp.sum(onehot * before, axis=0, keepdims=True).astype(jnp.int32))
        base = base + jnp.sum(onehot, axis=1, keepdims=True)
    cnt_ref[...] = jnp.broadcast_to(base, cnt_ref.shape)
    counts_ref[...] = cnt_ref[...]

    out_row = lax.broadcasted_iota(jnp.int32, ridx_ref.shape, 0)
    ridx = jnp.zeros(ridx_ref.shape, jnp.int32)
    rgate = jnp.zeros(rgate_ref.shape, F32)
    for kk in range(TOP_K):
        ridx = jnp.where(out_row == kk, idxs[kk], ridx)
        ridx = jnp.where(out_row == TOP_K + kk, ranks[kk], ridx)
        rgate = jnp.where(out_row == kk, exps[kk] / denom, rgate)
    ridx_ref[...] = ridx
    rgate_ref[...] = rgate


def _mix(x2, o_f, o_b, h_f, h_b, proj, w_out_bf16, hg_norm3, lru_norm3, ln_g3, ln_b3, router_w_pad, router_b3_pad,
         n_exp, layer, alpha, zg_block, zy_block):
    t, d = x2.shape
    w = o_f.shape[1]
    assert router_w_pad.shape[-1] == LANES and n_exp <= LANES and n_exp % SUBLANES == 0
    tm = min(512, t)
    row_blk = lambda width: pl.BlockSpec((tm, width), lambda i: (i, 0))
    col_blk = pl.BlockSpec((2 * TOP_K, tm), lambda i: (0, i))
    vec = lambda width: pl.BlockSpec((None, 1, width), lambda i: (layer, 0, 0))
    return pl.pallas_call(
        functools.partial(_mix_kernel, alpha=alpha, n_exp=n_exp),
        grid=(t // tm,),
        in_specs=[
            row_blk(d), row_blk(w), row_blk(w),
            pl.BlockSpec((tm, w), lambda i: (i, zg_block)),
            row_blk(w), row_blk(w),
            pl.BlockSpec((tm, w), lambda i: (i, zy_block)),
            pl.BlockSpec((None, 2 * w, d), lambda i: (layer, 0, 0)),
            vec(w), vec(w), vec(d), vec(d),
            pl.BlockSpec((None, d, LANES), lambda i: (layer, 0, 0)),
            vec(LANES),
            pl.BlockSpec((tm, tm), lambda i: (0, 0)),
        ],
        out_specs=[pl.BlockSpec((tm * d // LANES, LANES), lambda i: (i, 0)), col_blk, col_blk,
                   pl.BlockSpec((n_exp, LANES), lambda i: (0, 0))],
        out_shape=[
            jax.ShapeDtypeStruct((t * d // LANES, LANES), F32),
            jax.ShapeDtypeStruct((2 * TOP_K, t), jnp.int32),
            jax.ShapeDtypeStruct((2 * TOP_K, t), F32),
            jax.ShapeDtypeStruct((n_exp, LANES), F32),
        ],
        scratch_shapes=[pltpu.VMEM((n_exp, LANES), F32)],
        compiler_params=_params("arbitrary"),
        name="mix_ln_router",
    )(x2, o_f, o_b, proj, h_f, h_b, proj, w_out_bf16, hg_norm3, lru_norm3, ln_g3, ln_b3, router_w_pad,
      router_b3_pad, jnp.triu(jnp.ones((tm, tm), BF16), k=1))


DMA_UNROLL_ROWS = 32


def _routing_tables(ridx, counts_f32, n_exp, block_rows):
    t = ridx.shape[1]
    idx = ridx[:TOP_K].T
    rank = ridx[TOP_K:2 * TOP_K].T
    counts = counts_f32[:, 0].astype(jnp.int32)
    padded = ((counts + block_rows - 1) // block_rows) * block_rows
    pends = jnp.cumsum(padded)
    pstarts = pends - padded
    onehot = idx[:, :, None] == jnp.arange(n_exp, dtype=jnp.int32)[None, None, :]
    dest = (jnp.sum(jnp.where(onehot, pstarts[None, None, :], 0), axis=-1) + rank).reshape(-1).astype(jnp.int32)
    n_rows = t * TOP_K + n_exp * block_rows
    n_blocks = n_rows // block_rows
    blk_start = jnp.arange(n_blocks, dtype=jnp.int32) * block_rows
    blk_e = jnp.minimum(jnp.searchsorted(pends, blk_start, side="right", method="compare_all"),
                        n_exp - 1).astype(jnp.int32)
    n_used = (pends[-1] // block_rows).astype(jnp.int32).reshape(1)
    return dest, counts, pstarts.astype(jnp.int32), blk_e, n_used, n_rows


def _dispatch_kernel(cnt_ref, pst_ref, dest_ref, x_ref, xs_ref, zero_ref, sem, *, block_rows):
    per_row = xs_ref.shape[1]
    tm = x_ref.shape[0] // per_row

    def issue(g, carry):
        r0 = pl.multiple_of(g * DMA_UNROLL_ROWS, DMA_UNROLL_ROWS)
        for j in range(DMA_UNROLL_ROWS):
            src = x_ref.at[pl.ds(pl.multiple_of((r0 + j) * per_row, per_row), per_row)]
            for kk in range(TOP_K):
                d = dest_ref[(r0 + j) * TOP_K + kk]
                pltpu.make_async_copy(src, xs_ref.at[d], sem).start(priority=kk % 2)
        return carry

    lax.fori_loop(0, tm // DMA_UNROLL_ROWS, issue, 0)
    for _ in range(TOP_K):
        pltpu.make_async_copy(xs_ref.at[pl.ds(0, tm)], xs_ref.at[pl.ds(0, tm)], sem).wait()

    @pl.when(pl.program_id(0) == pl.num_programs(0) - 1)
    def _():
        zero_ref[...] = jnp.zeros_like(zero_ref)
        n_exp = cnt_ref.shape[0]
        last = n_exp - 1
        cnt_last = cnt_ref[last]
        used_rows = pst_ref[last] + cnt_last + (block_rows - cnt_last % block_rows) % block_rows
        n_tail = (xs_ref.shape[0] - used_rows) // block_rows

        def tail_copy(b):
            start = pl.multiple_of(used_rows + b * block_rows, block_rows)
            return pltpu.make_async_copy(zero_ref, xs_ref.at[pl.ds(start, block_rows)], sem)

        def tail_start(b, c2):
            tail_copy(b).start()
            return c2

        def tail_wait(b, c2):
            tail_copy(b).wait()
            return c2

        lax.fori_loop(0, n_tail, tail_start, 0)
        lax.fori_loop(0, n_tail, tail_wait, 0)

        def per_expert(e, carry):
            cnt = cnt_ref[e]
            first = pst_ref[e] + cnt
            n_pad = (block_rows - cnt % block_rows) % block_rows

            def pieces(action):
                off = first
                for bit in range(block_rows.bit_length() - 1):
                    size = 1 << bit
                    copy = pltpu.make_async_copy(zero_ref.at[pl.ds(0, size)], xs_ref.at[pl.ds(off, size)], sem)
                    pl.when((n_pad & size) != 0)(functools.partial(action, copy))
                    off = off + (n_pad & size)

            pieces(lambda copy: copy.start())
            pieces(lambda copy: copy.wait())
            return carry

        lax.fori_loop(0, n_exp, per_expert, 0)


def _dispatch(x1_tiles, dest, counts, pstarts, n_rows, block_rows, d):
    per_row = d // LANES
    t = x1_tiles.shape[0] // per_row
    tm = min(512, t)
    assert tm % DMA_UNROLL_ROWS == 0 and block_rows & (block_rows - 1) == 0
    return pl.pallas_call(
        functools.partial(_dispatch_kernel, block_rows=block_rows),
        grid_spec=pltpu.PrefetchScalarGridSpec(
            num_scalar_prefetch=2,
            grid=(t // tm,),
            in_specs=[
                pl.BlockSpec((tm * TOP_K,), lambda i, c, p: (i,), memory_space=pltpu.SMEM),
                pl.BlockSpec((tm * per_row, LANES), lambda i, c, p: (i, 0)),
            ],
            out_specs=pl.BlockSpec(memory_space=pl.ANY),
            scratch_shapes=[pltpu.VMEM((block_rows, per_row, LANES), F32), pltpu.SemaphoreType.DMA],
        ),
        out_shape=jax.ShapeDtypeStruct((n_rows, per_row, LANES), F32),
        compiler_params=_params("arbitrary"),
        name="moe_dispatch",
    )(counts, pstarts, dest, x1_tiles)


def _moe_ffn_kernel(blk_e_ref, n_used_ref, xs_ref, wgu_f32_ref, bgu_ref, wdn_f32_ref, bdn_ref, ys_ref,
                    wgu_ref, wdn_ref):
    i = pl.program_id(0)
    used = i < n_used_ref[0]

    @pl.when(used & ((i == 0) | (blk_e_ref[i] != blk_e_ref[jnp.maximum(i - 1, 0)])))
    def _():
        rows = LANES

        def cast_gu(j, carry):
            sl = pl.ds(pl.multiple_of(j * rows, rows), rows)
            wgu_ref[sl, :] = wgu_f32_ref[sl, :].astype(BF16)
            return carry

        def cast_dn(j, carry):
            sl = pl.ds(pl.multiple_of(j * rows, rows), rows)
            wdn_ref[sl, :] = wdn_f32_ref[sl, :].astype(BF16)
            return carry

        lax.fori_loop(0, wgu_ref.shape[0] // rows, cast_gu, 0)
        lax.fori_loop(0, wdn_ref.shape[0] // rows, cast_dn, 0)

    @pl.when(used)
    def _():
        de = wdn_ref.shape[0]
        bm = xs_ref.shape[0] * LANES // wgu_ref.shape[0]
        x = _load_row_tiles(xs_ref, bm).astype(BF16)
        gu = jnp.dot(x, wgu_ref[...], preferred_element_type=F32) + bgu_ref[...]
        gate = jnp.minimum(gu[:, :de], SWIGLU_LIMIT)
        up = jnp.clip(gu[:, de:], -SWIGLU_LIMIT, SWIGLU_LIMIT)
        hid = (up + 1.0) * (gate * jax.nn.sigmoid(SWIGLU_ALPHA * gate))
        _store_row_tiles(ys_ref, jnp.dot(hid.astype(BF16), wdn_ref[...], preferred_element_type=F32) + bdn_ref[...])

    @pl.when(jnp.logical_not(used))
    def _():
        ys_ref[...] = jnp.zeros_like(ys_ref)


def _moe_ffn(xs_tiles, blk_e, n_used, w_gate_up, b_gu4, w_down, b_dn4, layer, block_rows):
    d, de = w_down.shape[3], w_down.shape[2]
    per_row = d // LANES
    n_blocks = xs_tiles.shape[0] // (block_rows * per_row)
    w_map = lambda i, be, nu: (layer, be[i], 0, 0)
    return pl.pallas_call(
        _moe_ffn_kernel,
        grid_spec=pltpu.PrefetchScalarGridSpec(
            num_scalar_prefetch=2,
            grid=(n_blocks,),
            in_specs=[
                pl.BlockSpec((block_rows * per_row, LANES), lambda i, be, nu: (jnp.minimum(i, nu[0] - 1), 0)),
                pl.BlockSpec((None, None, d, 2 * de), w_map),
                pl.BlockSpec((None, None, 1, 2 * de), w_map),
                pl.BlockSpec((None, None, de, d), w_map),
                pl.BlockSpec((None, None, 1, d), w_map),
            ],
            out_specs=pl.BlockSpec((block_rows * per_row, LANES), lambda i, be, nu: (i, 0)),
            scratch_shapes=[pltpu.VMEM((d, 2 * de), BF16), pltpu.VMEM((de, d), BF16)],
        ),
        out_shape=jax.ShapeDtypeStruct(xs_tiles.shape, F32),
        compiler_params=_params("arbitrary"),
        name="moe_ffn",
    )(blk_e, n_used, xs_tiles, w_gate_up, b_gu4, w_down, b_dn4)


def _combine_kernel(dest_ref, gate_ref, x1_ref, ys_ref, g2_ref, b2_ref, o_ref, buf0, buf1, sem, *, alpha):
    i = pl.program_id(0)
    n_tiles = pl.num_programs(0) - 1
    tm = o_ref.shape[0]
    per_row = ys_ref.shape[1]

    def issue(buf, slot):
        def body(g, carry):
            r0 = pl.multiple_of(g * DMA_UNROLL_ROWS, DMA_UNROLL_ROWS)
            for j in range(DMA_UNROLL_ROWS):
                rows = pl.ds(pl.multiple_of((r0 + j) * per_row, per_row), per_row)
                for kk in range(TOP_K):
                    src = dest_ref[(r0 + j) * TOP_K + kk]
                    pltpu.make_async_copy(ys_ref.at[src], buf.at[kk, rows], sem.at[slot]).start(priority=kk % 2)
            return carry

        lax.fori_loop(0, tm // DMA_UNROLL_ROWS, body, 0)

    def finish(buf, slot):
        for kk in range(TOP_K):
            pltpu.make_async_copy(ys_ref.at[pl.ds(0, tm)], ys_ref.at[pl.ds(0, tm)], sem.at[slot]).wait()
        gates = gate_ref[...]
        m = gates[:, 0:1] * _load_row_tiles(buf.at[0], tm)
        for kk in range(1, TOP_K):
            m = m + gates[:, kk:kk + 1] * _load_row_tiles(buf.at[kk], tm)
        o_ref[...] = _layernorm(alpha * _load_row_tiles(x1_ref, tm) + m, g2_ref[...], b2_ref[...])

    for parity, (cur, prev) in enumerate(((buf0, buf1), (buf1, buf0))):
        @pl.when(i % 2 == parity)
        def _():
            @pl.when(i < n_tiles)
            def _():
                issue(cur, parity)

            @pl.when(i > 0)
            def _():
                finish(prev, 1 - parity)


def _combine(x1_tiles, ys3, dest, rgate, ln_g3, ln_b3, layer, alpha):
    per_row = ys3.shape[1]
    d = per_row * LANES
    t = x1_tiles.shape[0] // per_row
    tm = min(256, t)
    nt = t // tm
    assert tm % DMA_UNROLL_ROWS == 0
    done = lambda i: (jnp.maximum(i - 1, 0), 0)
    row_buffer = pltpu.VMEM((TOP_K, tm * per_row, LANES), F32)
    return pl.pallas_call(
        functools.partial(_combine_kernel, alpha=alpha),
        grid=(nt + 1,),
        in_specs=[
            pl.BlockSpec((tm * TOP_K,), lambda i: (jnp.minimum(i, nt - 1),), memory_space=pltpu.SMEM),
            pl.BlockSpec((tm, rgate.shape[1]), done),
            pl.BlockSpec((tm * per_row, LANES), done),
            pl.BlockSpec(memory_space=pl.ANY),
            pl.BlockSpec((None, 1, d), lambda i: (layer, 0, 0)),
            pl.BlockSpec((None, 1, d), lambda i: (layer, 0, 0)),
        ],
        out_specs=pl.BlockSpec((tm, d), done),
        out_shape=jax.ShapeDtypeStruct((t, d), F32),
        scratch_shapes=[row_buffer, row_buffer, pltpu.SemaphoreType.DMA((2,))],
        compiler_params=_params("arbitrary"),
        name="moe_combine",
    )(dest, rgate, x1_tiles, ys3, ln_g3, ln_b3)


def _block_diag_gate_weights(wa, wx):
    n_l, n_dir, n_h, hd, _ = wa.shape
    hh = n_h // 2
    eye = jnp.eye(hh, dtype=wa.dtype)

    def bd(wm):
        wm = wm.reshape(n_l, n_dir, 2, hh, hd, hd)
        full = jnp.einsum("ldghij,hk->ldghikj", wm, eye)
        return full.reshape(n_l, n_dir, 2, hh * hd, hh * hd)

    return jnp.concatenate([bd(wa), bd(wx)], axis=-1).astype(BF16)


def kernel(x, w_in, hg_lb, hg_norm, lru_conv_w, lru_conv_b, lru_wa, lru_ba, lru_wx, lru_bx, lru_lambda, lru_norm,
           w_out, ln1_g, ln1_b, router_w, router_b, w_gate_up, b_gate_up, w_down, b_down, ln2_g, ln2_b):
    bsz, s, d = x.shape
    depth = w_in.shape[0]
    hg_w = hg_lb.shape[-1]
    lru_w = lru_lambda.shape[-1]
    n_exp = router_w.shape[-1]
    de = w_down.shape[2]
    t = bsz * s
    alpha = float((2 * depth) ** 0.25)
    block_rows = 2 * MXU_DIM
    assert w_in.shape[-1] == 5 * hg_w + 2 * lru_w and hg_w == lru_w
    assert s % 512 == 0 or s < 512 and s % HG_CHUNK == 0

    p = jax.nn.softmax(hg_lb.astype(F32), axis=0)
    lower_bounds = jnp.clip(jnp.cumsum(p, axis=0) - p[0:1], 0.0, 1.0 - 1e-6)
    w_in_b = w_in.astype(BF16)
    w_out_b = w_out.astype(BF16)
    wg = _block_diag_gate_weights(lru_wa, lru_wx)
    gate_bias = jnp.stack([lru_ba, lru_bx], axis=2).astype(F32)
    row3 = lambda a: a.astype(F32).reshape(depth, 1, a.shape[-1])
    pad_experts = lambda a: jnp.pad(a.astype(F32), [(0, 0)] * (a.ndim - 1) + [(0, LANES - n_exp)])
    router_w_pad = pad_experts(router_w)
    router_b_pad = pad_experts(row3(router_b))
    b_gu4 = b_gate_up.astype(F32).reshape(depth, n_exp, 1, 2 * de)
    b_dn4 = b_down.astype(F32).reshape(depth, n_exp, 1, d)

    x2 = x.reshape(t, d)
    for layer in range(depth):
        proj = _in_proj(x2, w_in_b, layer)
        proj3 = proj.reshape(bsz, s, proj.shape[-1])
        o_f, o_b = _hgrn2(proj3, lower_bounds, layer, hg_w)
        h_f, h_b = _griffin(proj3, lru_conv_w, row3(lru_conv_b), wg, gate_bias, lru_lambda, layer, lru_w,
                            col_block=5)
        x1, ridx, rgate, counts_f32 = _mix(x2, o_f.reshape(t, hg_w), o_b.reshape(t, hg_w), h_f.reshape(t, lru_w),
                               h_b.reshape(t, lru_w), proj, w_out_b, row3(hg_norm), row3(lru_norm), row3(ln1_g),
                               row3(ln1_b), router_w_pad, router_b_pad, n_exp, layer, alpha, zg_block=4, zy_block=6)
        dest, counts, pstarts, blk_e, n_used, n_rows = _routing_tables(ridx, counts_f32, n_exp, block_rows)
        xs = _dispatch(x1, dest, counts, pstarts, n_rows, block_rows, d)
        ys = _moe_ffn(xs.reshape(-1, LANES), blk_e, n_used, w_gate_up, b_gu4, w_down, b_dn4, layer, block_rows)
        x2 = _combine(x1, ys.reshape(xs.shape), dest, rgate.T, row3(ln2_g), row3(ln2_b), layer, alpha)
    return x2.reshape(bsz, s, d)
```
